```python
import math
import jax, jax.numpy as jnp
from jax import lax
import numpy as np

D_MODEL = 1024
BATCH = 4
SEQ = 8192
DEPTH = 2

BLOCK = 128
HEAD_DIM = 64
A_HEADS = 4
A_QK = 32
A_V = 2 * A_QK
B_QHEADS = 4
B_KVHEADS = 2
B_WINDOW = 128
C_PATTERNS = ((128, 1), (512, 4), (2048, 16))
C_HEADS = 4
D_HEADS = 4
D_TOPK_MAX = 256
IDX_HEADS = 8
IDX_DIM = 32
REL_BUCKETS = 32
REL_MAX_DIST = 2048
A_BIAS0 = 0
B_BIAS0 = A_BIAS0 + A_HEADS
C_BIAS0 = B_BIAS0 + B_QHEADS
D_BIAS0 = C_BIAS0 + len(C_PATTERNS) * C_HEADS
N_BIAS_HEADS = D_BIAS0 + D_HEADS
N_BRANCH = 4
BRANCH_WIDTH = 256
N_EXPERTS = 16
N_GROUPS = 4
EXPERTS_PER_GROUP = N_EXPERTS // N_GROUPS
TOP_K = 2
GROUP_SCORE_TOPK = 2
D_FF_EXPERT = 256
MOE_BLOCK = 128
DEEPNORM_ALPHA = (2 * DEPTH) ** 0.25
DEEPNORM_BETA = (8 * DEPTH) ** -0.25

IN_SPLITS = (
    ('a_q', A_HEADS * 2 * A_QK), ('a_k', A_HEADS * 2 * A_QK), ('a_v', A_HEADS * A_V),
    ('b_q', B_QHEADS * HEAD_DIM), ('b_k', B_KVHEADS * HEAD_DIM), ('b_v', B_KVHEADS * HEAD_DIM),
    ('c_q', len(C_PATTERNS) * C_HEADS * HEAD_DIM), ('c_k', len(C_PATTERNS) * C_HEADS * HEAD_DIM),
    ('c_v', len(C_PATTERNS) * C_HEADS * HEAD_DIM),
    ('d_q', D_HEADS * HEAD_DIM), ('d_k', HEAD_DIM), ('d_v', HEAD_DIM),
    ('i_q', IDX_HEADS * IDX_DIM), ('i_k', IDX_DIM), ('i_w', IDX_HEADS),
    ('gate', N_BRANCH * D_MODEL),
)
IN_WIDTH = sum(w for _, w in IN_SPLITS)

kernel_name = 'hybrid_gated_four_mixer_deepnorm_moe'


def layer_norm(x, g, b, eps=1e-5):
    xf = x.astype(jnp.float32)
    mu = jnp.mean(xf, axis=-1, keepdims=True)
    var = jnp.mean(jnp.square(xf - mu), axis=-1, keepdims=True)
    return ((xf - mu) * lax.rsqrt(var + eps)).astype(x.dtype) * g + b


def rms_norm(x, g, eps=1e-5):
    xf = x.astype(jnp.float32)
    return (xf * lax.rsqrt(jnp.mean(jnp.square(xf), axis=-1, keepdims=True) + eps)).astype(x.dtype) * g


def t5_bucket(n):
    max_exact = REL_BUCKETS // 2
    nf = jnp.maximum(n, 1).astype(jnp.float32)
    log_ratio = jnp.log(nf / max_exact) / math.log(REL_MAX_DIST / max_exact)
    large = max_exact + (log_ratio * (REL_BUCKETS - max_exact)).astype(jnp.int32)
    large = jnp.minimum(large, REL_BUCKETS - 1)
    return jnp.where(n < max_exact, n, large)


def rel_bias(rel_table, dist, h0, nh):
    return rel_table[:, h0:h0 + nh].astype(jnp.float32)[t5_bucket(jnp.maximum(dist, 0))]


def split_in(h):
    parts = {}
    off = 0
    for name, width in IN_SPLITS:
        parts[name] = h[..., off:off + width]
        off += width
    return parts


def diff_attention(q, k, v, lam_q, lam_k, subln_g, rel_table, layer_idx):
    bsz, seq, _ = q.shape
    nb = seq // BLOCK
    qb = jnp.moveaxis(q.reshape(bsz, nb, BLOCK, A_HEADS, 2, A_QK), 1, 0)
    k = k.reshape(bsz, seq, A_HEADS, 2, A_QK)
    v = v.reshape(bsz, seq, A_HEADS, A_V)
    lam_init = 0.8 - 0.6 * math.exp(-0.3 * layer_idx)
    lq = lam_q.astype(jnp.float32)
    lk = lam_k.astype(jnp.float32)
    lam = jnp.exp(jnp.sum(lq[0] * lk[0])) - jnp.exp(jnp.sum(lq[1] * lk[1])) + lam_init
    kpos = jnp.arange(seq)
    scale = A_QK ** -0.5

    def block(args):
        q_blk, i = args
        qpos = i * BLOCK + jnp.arange(BLOCK)
        dist = qpos[:, None] - kpos[None, :]
        bias = jnp.moveaxis(rel_bias(rel_table, dist, A_BIAS0, A_HEADS), -1, 0)
        s = jnp.einsum('bqhmd,bkhmd->bhmqk', q_blk, k).astype(jnp.float32) * scale + bias[None, :, None]
        s = jnp.where(dist >= 0, s, -jnp.inf)
        p = jax.nn.softmax(s, axis=-1)
        w = p[:, :, 0] - lam * p[:, :, 1]
        return jnp.einsum('bhqk,bkhe->bqhe', w.astype(v.dtype), v)

    o = lax.map(block, (qb, jnp.arange(nb)))
    o = jnp.moveaxis(o, 0, 1).reshape(bsz, seq, A_HEADS, A_V)
    o = rms_norm(o, subln_g) * (1.0 - lam_init)
    return o.reshape(bsz, seq, A_HEADS * A_V)


def sliding_sink_attention(q, k, v, sinks, rel_table):
    bsz, seq, _ = q.shape
    nb = seq // BLOCK
    rep = B_QHEADS // B_KVHEADS
    q = q.reshape(bsz, nb, BLOCK, B_KVHEADS, rep, HEAD_DIM)

    def band(a):
        a = a.reshape(bsz, nb, BLOCK, B_KVHEADS, HEAD_DIM)
        prev = jnp.pad(a[:, :-1], ((0, 0), (1, 0), (0, 0), (0, 0), (0, 0)))
        return jnp.concatenate([prev, a], axis=2)

    kb, vb = band(k), band(v)
    ql = jnp.arange(BLOCK)
    kl = jnp.arange(2 * BLOCK)
    dist = ql[:, None] + BLOCK - kl[None, :]
    bias = rel_bias(rel_table, dist, B_BIAS0, B_QHEADS).reshape(BLOCK, 2 * BLOCK, B_KVHEADS, rep)
    bias = jnp.transpose(bias, (2, 3, 0, 1))
    kpos = jnp.arange(nb)[:, None] * BLOCK - BLOCK + kl[None, :]
    valid = ((dist >= 0) & (dist < B_WINDOW))[None] & (kpos >= 0)[:, None, :]
    s = jnp.einsum('bnqgrd,bnkgd->bngrqk', q, kb).astype(jnp.float32) * HEAD_DIM ** -0.5 + bias[None, None]
    s = jnp.where(valid[None, :, None, None], s, -jnp.inf)
    sink = jnp.broadcast_to(
        sinks.astype(jnp.float32).reshape(B_KVHEADS, rep)[None, None, :, :, None, None], s.shape[:-1] + (1,))
    p = jax.nn.softmax(jnp.concatenate([s, sink], axis=-1), axis=-1)[..., :-1]
    o = jnp.einsum('bngrqk,bnkgd->bnqgrd', p.astype(vb.dtype), vb)
    return o.reshape(bsz, seq, B_QHEADS * HEAD_DIM)


def dilated_attention(q, k, v, rel_table):
    bsz, seq, _ = q.shape
    nb = seq // BLOCK
    n_pat = len(C_PATTERNS)
    q = q.reshape(bsz, seq, n_pat, C_HEADS, HEAD_DIM)
    k = k.reshape(bsz, seq, n_pat, C_HEADS, HEAD_DIM)
    v = v.reshape(bsz, seq, n_pat, C_HEADS, HEAD_DIM)
    k_groups = [k[:, :, g] for g in range(n_pat)]
    v_groups = [v[:, :, g] for g in range(n_pat)]
    scale = HEAD_DIM ** -0.5

    def block(i):
        qpos = i * BLOCK + jnp.arange(BLOCK)
        q_blk = lax.dynamic_slice_in_dim(q, i * BLOCK, BLOCK, axis=1)
        outs, lses = [], []
        for g, (window, dil) in enumerate(C_PATTERNS):
            dist = dil * jnp.arange(window // dil + 1)
            idx = qpos[:, None] - dist[None, :]
            valid = idx >= 0
            idx = jnp.maximum(idx, 0)
            kg = k_groups[g][:, idx]
            vg = v_groups[g][:, idx]
            bias = rel_bias(rel_table, dist, C_BIAS0 + g * C_HEADS, C_HEADS).T
            s = jnp.einsum('bqhd,bqjhd->bqhj', q_blk[:, :, g], kg).astype(jnp.float32) * scale + bias
            s = jnp.where(valid[None, :, None, :], s, -jnp.inf)
            m = jnp.max(s, axis=-1, keepdims=True)
            e = jnp.exp(s - m)
            den = jnp.sum(e, axis=-1)
            outs.append(jnp.einsum('bqhj,bqjhd->bqhd', e, vg.astype(jnp.float32)) / den[..., None])
            lses.append(m[..., 0] + jnp.log(den))
        alpha = jax.nn.softmax(jnp.stack(lses, 0), axis=0)
        return jnp.sum(alpha[..., None] * jnp.stack(outs, 0), axis=0)

    o = lax.map(block, jnp.arange(nb))
    return jnp.moveaxis(o, 0, 1).reshape(bsz, seq, C_HEADS * HEAD_DIM).astype(v.dtype)


def indexed_sparse_attention(q, k, v, iq, ik, iw, rel_table):
    bsz, seq, _ = q.shape
    nb = seq // BLOCK
    k_sel = min(D_TOPK_MAX, seq // 4)
    q = q.reshape(bsz, seq, D_HEADS, HEAD_DIM)
    iq = iq.reshape(bsz, seq, IDX_HEADS, IDX_DIM)
    kpos = jnp.arange(seq)
    gather = jax.vmap(lambda a, idx: a[idx])
    scale = HEAD_DIM ** -0.5

    def block(i):
        qpos = i * BLOCK + jnp.arange(BLOCK)
        q_blk = lax.dynamic_slice_in_dim(q, i * BLOCK, BLOCK, axis=1)
        iq_blk = lax.dynamic_slice_in_dim(iq, i * BLOCK, BLOCK, axis=1)
        iw_blk = lax.dynamic_slice_in_dim(iw, i * BLOCK, BLOCK, axis=1)
        rel = jax.nn.relu(jnp.einsum('bqhd,bsd->bqhs', iq_blk, ik).astype(jnp.float32))
        index_score = jnp.einsum('bqhs,bqh->bqs', rel, iw_blk.astype(jnp.float32))
        index_score = jnp.where(kpos[None, None, :] <= qpos[None, :, None], index_score, -jnp.inf)
        _, sel = lax.top_k(index_score, k_sel)
        dist = qpos[None, :, None] - sel
        kg = gather(k, sel)
        vg = gather(v, sel)
        bias = jnp.moveaxis(rel_bias(rel_table, dist, D_BIAS0, D_HEADS), -1, 2)
        s = jnp.einsum('bqhd,bqkd->bqhk', q_blk, kg).astype(jnp.float32) * scale + bias
        s = jnp.where((dist >= 0)[:, :, None, :], s, -jnp.inf)
        p = jax.nn.softmax(s, axis=-1)
        return jnp.einsum('bqhk,bqkd->bqhd', p.astype(vg.dtype), vg)

    o = lax.map(block, jnp.arange(nb))
    return jnp.moveaxis(o, 0, 1).reshape(bsz, seq, D_HEADS * HEAD_DIM)


def hybrid_mixer(u, w_in, lam_q, lam_k, subln_g, sinks, w_branch, w_out, rel_table, layer_idx):
    bsz, seq, dm = u.shape
    parts = split_in(u @ w_in)
    o_a = diff_attention(parts['a_q'], parts['a_k'], parts['a_v'], lam_q, lam_k, subln_g, rel_table, layer_idx)
    o_b = sliding_sink_attention(parts['b_q'], parts['b_k'], parts['b_v'], sinks, rel_table)
    o_c = dilated_attention(parts['c_q'], parts['c_k'], parts['c_v'], rel_table)
    o_d = indexed_sparse_attention(parts['d_q'], parts['d_k'], parts['d_v'],
                                   parts['i_q'], parts['i_k'], parts['i_w'], rel_table)
    gates = jax.nn.sigmoid(parts['gate'].astype(jnp.float32)).reshape(bsz, seq, N_BRANCH, dm)
    branches = (o_a, o_b, o_c, o_d)
    merged = gates[:, :, 0] * (o_a @ w_branch[0])
    for i in range(1, N_BRANCH):
        merged = merged + gates[:, :, i] * (branches[i] @ w_branch[i])
    return (merged.astype(u.dtype) @ w_out).astype(u.dtype)


def routed_moe(v, w_router, b_router, w_exp_gate, w_exp_up, w_exp_down):
    bsz, seq, dm = v.shape
    n_tok = bsz * seq
    vf = v.reshape(n_tok, dm)
    probs = jax.nn.softmax((vf @ w_router).astype(jnp.float32), axis=-1)
    sel_score = probs + b_router.astype(jnp.float32)
    grp_score = jnp.sum(lax.top_k(sel_score.reshape(n_tok, N_GROUPS, EXPERTS_PER_GROUP), GROUP_SCORE_TOPK)[0], axis=-1)
    top_group = jnp.argmax(grp_score, axis=-1)
    in_group = (jnp.arange(N_EXPERTS) // EXPERTS_PER_GROUP)[None, :] == top_group[:, None]
    _, top_idx = lax.top_k(jnp.where(in_group, sel_score, -jnp.inf), TOP_K)
    top_w = jnp.take_along_axis(probs, top_idx, axis=-1)
    top_w = top_w / jnp.sum(top_w, axis=-1, keepdims=True)
    n_asg = n_tok * TOP_K
    e_a = top_idx.reshape(n_asg).astype(jnp.int32)
    tok_a = jnp.arange(n_asg, dtype=jnp.int32) // TOP_K
    w_a = top_w.reshape(n_asg)
    order = jnp.argsort(e_a)
    e_s, tok_s, w_s = e_a[order], tok_a[order], w_a[order]
    counts = jnp.zeros((N_EXPERTS,), jnp.int32).at[e_a].add(1)
    padded = (counts + MOE_BLOCK - 1) // MOE_BLOCK * MOE_BLOCK
    pad_end = jnp.cumsum(padded)
    pad_start = pad_end - padded
    start = jnp.cumsum(counts) - counts
    dest = pad_start[e_s] + jnp.arange(n_asg, dtype=jnp.int32) - start[e_s]
    n_blocks = -(-n_asg // MOE_BLOCK) + N_EXPERTS
    n_slots = n_blocks * MOE_BLOCK
    slot_tok = jnp.full((n_slots,), n_tok, jnp.int32).at[dest].set(tok_s)
    slot_w = jnp.zeros((n_slots,), jnp.float32).at[dest].set(w_s)
    block_e = jnp.minimum(jnp.searchsorted(pad_end, jnp.arange(n_blocks) * MOE_BLOCK, side='right'), N_EXPERTS - 1)
    vpad = jnp.concatenate([vf, jnp.zeros((1, dm), vf.dtype)], axis=0)
    xb = vpad[slot_tok].reshape(n_blocks, MOE_BLOCK, dm)

    def expert_block(args):
        x_blk, e = args
        h = jax.nn.silu(x_blk @ w_exp_gate[e]) * (x_blk @ w_exp_up[e])
        return h @ w_exp_down[e]

    yb = lax.map(expert_block, (xb, block_e)).reshape(n_slots, dm)
    y = jnp.zeros((n_tok + 1, dm), jnp.float32).at[slot_tok].add(yb.astype(jnp.float32) * slot_w[:, None])
    return y[:n_tok].reshape(bsz, seq, dm).astype(v.dtype)


def setup_inputs(seed: int = 0) -> dict:
    key = jax.random.key(seed)
    ks = jax.random.split(key, 21)
    L, D, E, F = DEPTH, D_MODEL, N_EXPERTS, D_FF_EXPERT

    def nrm(k, shape, s):
        return jax.random.normal(k, shape, jnp.float32) * s

    return {
        'x': nrm(ks[0], (BATCH, SEQ, D), 1.0),
        'c': nrm(ks[1], (BATCH, D), 1.0),
        'rel_table': nrm(ks[2], (REL_BUCKETS, N_BIAS_HEADS), 0.5),
        'w_router': nrm(ks[3], (D, E), D ** -0.5),
        'b_router': nrm(ks[4], (E,), 0.01),
        'w_ada': nrm(ks[5], (L, D, 6 * D), 0.1 * D ** -0.5),
        'b_ada': nrm(ks[6], (L, 6 * D), 0.01),
        'w_in': nrm(ks[7], (L, D, IN_WIDTH), D ** -0.5),
        'a_lambda_q': nrm(ks[8], (L, 2, A_QK), 0.1),
        'a_lambda_k': nrm(ks[9], (L, 2, A_QK), 0.1),
        'a_subln_g': 1.0 + nrm(ks[10], (L, A_V), 0.01),
        'b_sinks': nrm(ks[11], (L, B_QHEADS), 0.5),
        'w_branch': nrm(ks[12], (L, N_BRANCH, BRANCH_WIDTH, D), BRANCH_WIDTH ** -0.5),
        'w_out': nrm(ks[13], (L, D, D), DEEPNORM_BETA * D ** -0.5),
        'ln1_g': 1.0 + nrm(ks[14], (L, D), 0.01),
        'ln1_b': nrm(ks[15], (L, D), 0.01),
        'w_exp_gate': nrm(ks[16], (L, E, D, F), D ** -0.5),
        'w_exp_up': nrm(ks[17], (L, E, D, F), D ** -0.5),
        'w_exp_down': nrm(ks[18], (L, E, F, D), DEEPNORM_BETA * F ** -0.5),
        'ln2_g': 1.0 + nrm(ks[19], (L, D), 0.01),
        'ln2_b': nrm(ks[20], (L, D), 0.01),
    }


def reference(x, c, rel_table, w_router, b_router, w_ada, b_ada, w_in, a_lambda_q, a_lambda_k,
              a_subln_g, b_sinks, w_branch, w_out, ln1_g, ln1_b, w_exp_gate, w_exp_up, w_exp_down,
              ln2_g, ln2_b):
    cond = jax.nn.silu(c)
    for l in range(DEPTH):
        mod = (cond @ w_ada[l] + b_ada[l])[:, None, :]
        shift_a, scale_a, gate_a, shift_f, scale_f, gate_f = jnp.split(mod, 6, axis=-1)
        u = x * (1.0 + scale_a) + shift_a
        y = hybrid_mixer(u, w_in[l], a_lambda_q[l], a_lambda_k[l], a_subln_g[l], b_sinks[l],
                         w_branch[l], w_out[l], rel_table, l)
        x = layer_norm(DEEPNORM_ALPHA * x + (1.0 + gate_a) * y, ln1_g[l], ln1_b[l])
        v = x * (1.0 + scale_f) + shift_f
        y = routed_moe(v, w_router, b_router, w_exp_gate[l], w_exp_up[l], w_exp_down[l])
        x = layer_norm(DEEPNORM_ALPHA * x + (1.0 + gate_f) * y, ln2_g[l], ln2_b[l])
    return x
```

```python
import functools
import math

import numpy as np
import jax
import jax.numpy as jnp
from jax import lax
from jax.experimental import pallas as pl
from jax.experimental.pallas import tpu as pltpu

f32 = jnp.float32
bf16 = jnp.bfloat16
i32 = jnp.int32

D_MODEL = 1024
DEPTH = 2
HEAD_DIM = 64
A_HEADS = 4
A_QK = 32
A_V = 2 * A_QK
B_QHEADS = 4
B_KVHEADS = 2
B_WINDOW = 128
C_PATTERNS = ((128, 1), (512, 4), (2048, 16))
C_HEADS = 4
D_HEADS = 4
D_TOPK_MAX = 256
IDX_HEADS = 8
IDX_DIM = 32
REL_BUCKETS = 32
REL_MAX_DIST = 2048
A_BIAS0 = 0
B_BIAS0 = A_BIAS0 + A_HEADS
C_BIAS0 = B_BIAS0 + B_QHEADS
D_BIAS0 = C_BIAS0 + len(C_PATTERNS) * C_HEADS
N_BIAS_HEADS = D_BIAS0 + D_HEADS
N_BRANCH = 4
BRANCH_WIDTH = 256
N_EXPERTS = 16
N_GROUPS = 4
EXPERTS_PER_GROUP = N_EXPERTS // N_GROUPS
D_FF_EXPERT = 256
DEEPNORM_ALPHA = (2 * DEPTH) ** 0.25
LN_EPS = 1e-5

IN_SPLITS = (
    ('a_q', A_HEADS * 2 * A_QK), ('a_k', A_HEADS * 2 * A_QK), ('a_v', A_HEADS * A_V),
    ('b_q', B_QHEADS * HEAD_DIM), ('b_k', B_KVHEADS * HEAD_DIM), ('b_v', B_KVHEADS * HEAD_DIM),
    ('c_q', len(C_PATTERNS) * C_HEADS * HEAD_DIM), ('c_k', len(C_PATTERNS) * C_HEADS * HEAD_DIM),
    ('c_v', len(C_PATTERNS) * C_HEADS * HEAD_DIM),
    ('d_q', D_HEADS * HEAD_DIM), ('d_k', HEAD_DIM), ('d_v', HEAD_DIM),
    ('i_q', IDX_HEADS * IDX_DIM), ('i_k', IDX_DIM), ('i_w', IDX_HEADS),
    ('gate', N_BRANCH * D_MODEL),
)

LANES = 128
SUBLANES = 8
T = 256
BAND = 128
BAND_TILE = 512
VMEM_LIMIT = 56 * 1024 * 1024
N_FAR = -(-(REL_MAX_DIST + T - 1) // T)

NEG_INIT = -1e30
NEG_MASK = -2e30
INT_MIN = -2 ** 31


def _params(n_axes):
    return pltpu.CompilerParams(dimension_semantics=("arbitrary",) * n_axes,
                                vmem_limit_bytes=VMEM_LIMIT)


def _const_spec(shape):
    nd = len(shape)
    return pl.BlockSpec(shape, lambda *_: (0,) * nd, pipeline_mode=pl.Buffered(1))


def _modulate(x_ref, shift_ref, scale_ref):
    return x_ref[0] * (1.0 + scale_ref[0]) + shift_ref[0]


def _layer_norm(z, g, b):
    mu = jnp.mean(z, axis=-1, keepdims=True)
    zc = z - mu
    var = jnp.mean(zc * zc, axis=-1, keepdims=True)
    return zc * lax.rsqrt(var + LN_EPS) * g + b


def _ada_kernel(c_ref, w_ref, b_ref, o_ref):
    c = c_ref[...]
    cond = c * jax.nn.sigmoid(c)
    o_ref[0] = jnp.dot(cond, w_ref[0], preferred_element_type=f32,
                       precision=lax.Precision.HIGHEST) + b_ref[0]


def _ada_call(c, w_ada, b_ada):
    depth, d, n = w_ada.shape
    bsz = c.shape[0]
    rows = -(-bsz // SUBLANES) * SUBLANES
    c_pad = jnp.zeros((rows, d), f32).at[:bsz].set(c)
    tn = 1536
    out = pl.pallas_call(
        _ada_kernel, name="ada_mod",
        grid=(depth, n // tn),
        in_specs=[pl.BlockSpec((rows, d), lambda l, j: (0, 0)),
                  pl.BlockSpec((1, d, tn), lambda l, j: (l, 0, j)),
                  pl.BlockSpec((1, 1, tn), lambda l, j: (l, 0, j))],
        out_specs=pl.BlockSpec((1, rows, tn), lambda l, j: (l, 0, j)),
        out_shape=jax.ShapeDtypeStruct((depth, rows, n), f32),
        compiler_params=_params(2),
    )(c_pad, w_ada, b_ada.reshape(depth, 1, n))
    return out[:, :bsz]


def _proj_kernel(*refs, kinds):
    n = len(kinds)
    x_ref, shift_ref, scale_ref = refs[:3]
    w_refs = refs[3:3 + n]
    o_refs = refs[3 + n:]
    u = _modulate(x_ref, shift_ref, scale_ref).astype(bf16)
    for kind, w_ref, o_ref in zip(kinds, w_refs, o_refs):
        if kind[0] == 'feat':
            r = lax.dot_general(w_ref[...], u, (((1,), (1,)), ((), ())), preferred_element_type=f32)
            o_ref[0, 0] = r.astype(o_ref.dtype)
        elif kind[0] == 'tok':
            r = jnp.dot(u, w_ref[...], preferred_element_type=f32)
            o_ref[0] = r.astype(o_ref.dtype)
        else:
            _, heads, hd = kind
            r = jnp.dot(u, w_ref[...], preferred_element_type=f32)
            for h in range(heads):
                o_ref[0, h] = r[:, h * hd:(h + 1) * hd].astype(o_ref.dtype)


def _proj_call(name, x, shift, scale, w_cols, kinds, dtypes):
    bsz, seq, d = x.shape
    nt = seq // T
    weights, w_specs, out_specs, out_shapes = [], [], [], []
    for w, kind, dt in zip(w_cols, kinds, dtypes):
        feat = w.shape[1]
        if kind[0] == 'feat':
            weights.append(w.T)
            w_specs.append(_const_spec((feat, d)))
            out_specs.append(pl.BlockSpec((1, 1, feat, T), lambda b, i: (b, i, 0, 0)))
            out_shapes.append(jax.ShapeDtypeStruct((bsz, nt, feat, T), dt))
        elif kind[0] == 'tok':
            weights.append(w)
            w_specs.append(_const_spec((d, feat)))
            out_specs.append(pl.BlockSpec((1, T, feat), lambda b, i: (b, i, 0)))
            out_shapes.append(jax.ShapeDtypeStruct((bsz, seq, feat), dt))
        else:
            _, heads, hd = kind
            weights.append(w)
            w_specs.append(_const_spec((d, feat)))
            out_specs.append(pl.BlockSpec((1, heads, T, hd), lambda b, i: (b, 0, i, 0)))
            out_shapes.append(jax.ShapeDtypeStruct((bsz, heads, seq, hd), dt))
    mod_spec = pl.BlockSpec((1, 1, d), lambda b, i: (b, 0, 0))
    return pl.pallas_call(
        functools.partial(_proj_kernel, kinds=tuple(kinds)), name=name,
        grid=(bsz, nt),
        in_specs=[pl.BlockSpec((1, T, d), lambda b, i: (b, i, 0)), mod_spec, mod_spec] + w_specs,
        out_specs=out_specs, out_shape=out_shapes,
        compiler_params=_params(2),
    )(x, shift, scale, *weights)


def _softmax_step(s, v_t, state):
    mx, l, acc = state
    mx_new = jnp.maximum(mx, jnp.max(s, axis=0, keepdims=True))
    alpha = jnp.exp(mx - mx_new)
    p = jnp.exp(s - mx_new)
    l = alpha * l + jnp.sum(p, axis=0, keepdims=True)
    acc = alpha * acc + jnp.dot(v_t, p.astype(bf16), preferred_element_type=f32)
    return mx_new, l, acc


def _softmax_init(dv):
    return (jnp.full((1, T), NEG_INIT, f32), jnp.zeros((1, T), f32), jnp.zeros((dv, T), f32))


def _causal_tile():
    key = lax.broadcasted_iota(i32, (T, T), 0)
    qry = lax.broadcasted_iota(i32, (T, T), 1)
    return key <= qry


def _diff_attn_kernel(lq_ref, lk_ref, q_ref, k_ref, v_ref, bias_ref, g_ref, o_ref, *, lam_init):
    i = pl.program_id(2)
    q_t = q_ref[0, 0]
    q_parts = (q_t[:A_QK], q_t[A_QK:])
    scale = A_QK ** -0.5
    causal = _causal_tile()

    def block(j, state, diagonal):
        start = pl.multiple_of(j * T, T)
        v_t = v_ref[0, j]
        bias = bias_ref[0, jnp.minimum(i - j, N_FAR)]
        out = []
        for m in range(2):
            kb = k_ref[0, m, pl.ds(start, T), :]
            s = jnp.dot(kb, q_parts[m], preferred_element_type=f32) * scale + bias
            if diagonal:
                s = jnp.where(causal, s, NEG_MASK)
            out.append(_softmax_step(s, v_t, state[m]))
        return tuple(out)

    state = (_softmax_init(A_V), _softmax_init(A_V))
    state = lax.fori_loop(0, i, lambda j, st: block(j, st, False), state)
    (_, l0, acc0), (_, l1, acc1) = block(i, state, True)

    lq = lq_ref[...]
    lk = lk_ref[...]
    lam = (jnp.exp(jnp.sum(lq[0:1] * lk[0:1], axis=1, keepdims=True))
           - jnp.exp(jnp.sum(lq[1:2] * lk[1:2], axis=1, keepdims=True)) + lam_init)
    o = acc0 / l0 - lam * (acc1 / l1)
    ms = jnp.mean(o * o, axis=0, keepdims=True)
    o = o * lax.rsqrt(ms + LN_EPS) * g_ref[...] * (1.0 - lam_init)
    o_ref[0, 0] = o.astype(o_ref.dtype)


def _diff_attn_call(q_t, k_h, v_t, bias, lam_q, lam_k, subln_g, layer_idx):
    bsz, nt, _, _ = q_t.shape
    seq = nt * T
    lam_init = 0.8 - 0.6 * math.exp(-0.3 * layer_idx)
    g = jnp.broadcast_to(subln_g.astype(f32)[:, None], (A_V, T))
    nb = bias.shape[1]
    return pl.pallas_call(
        functools.partial(_diff_attn_kernel, lam_init=lam_init), name="mixer_a_diff_attn",
        grid=(bsz, A_HEADS, nt),
        in_specs=[_const_spec((2, A_QK)), _const_spec((2, A_QK)),
                  pl.BlockSpec((1, 1, 2 * A_QK, T), lambda b, h, i: (b, i, h, 0)),
                  pl.BlockSpec((1, 2, seq, A_QK), lambda b, h, i: (b, h, 0, 0)),
                  pl.BlockSpec((1, nt, A_V, T), lambda b, h, i: (b, 0, h, 0)),
                  pl.BlockSpec((1, nb, T, T), lambda b, h, i: (h, 0, 0, 0)),
                  _const_spec((A_V, T))],
        out_specs=pl.BlockSpec((1, 1, A_V, T), lambda b, h, i: (b, i, h, 0)),
        out_shape=jax.ShapeDtypeStruct((bsz, nt, A_HEADS * A_V, T), bf16),
        compiler_params=_params(3),
    )(lam_q.astype(f32), lam_k.astype(f32), q_t, k_h, v_t, bias, g)


def _sparse_attn_kernel(q_ref, k_ref, v_ref, iq_ref, ik_ref, iw_ref, bias_ref, o_ref, key_ref,
                        *, k_sel, seq):
    i = pl.program_id(1)
    causal = _causal_tile()
    key_iota = lax.broadcasted_iota(i32, (T, T), 0)
    iq_t = iq_ref[0, 0]
    iw_t = iw_ref[0, 0]

    def score_block(j, diagonal):
        start = pl.multiple_of(j * T, T)
        ikb = ik_ref[0, pl.ds(start, T), :]
        sc = jnp.zeros((T, T), f32)
        for h in range(IDX_HEADS):
            r = jnp.dot(ikb, iq_t[h * IDX_DIM:(h + 1) * IDX_DIM], preferred_element_type=f32)
            sc = sc + jnp.maximum(r, 0.0) * iw_t[h:h + 1]
        bits = pltpu.bitcast(sc, i32)
        key = bits ^ ((bits >> 31) & jnp.int32(0x7FFFFFFF))
        if diagonal:
            key = jnp.where(causal, key, jnp.int32(INT_MIN))
        key_ref[pl.ds(start, T), :] = key

    def _score_body(j, carry):
        score_block(j, False)
        return carry

    lax.fori_loop(0, i, _score_body, 0)
    score_block(i, True)

    def count(pred):
        def body(j, acc):
            start = pl.multiple_of(j * T, T)
            ind = jnp.where(pred(key_ref[pl.ds(start, T), :], start), 1, 0).astype(i32)
            return acc + ind.reshape(T // SUBLANES, SUBLANES, T).sum(axis=0)
        acc = lax.fori_loop(0, i + 1, body, jnp.zeros((SUBLANES, T), i32))
        return acc.sum(axis=0, keepdims=True)

    def bit_step(b, t_u):
        cand_u = t_u | lax.shift_left(jnp.int32(1), 31 - b)
        cand = cand_u ^ jnp.int32(INT_MIN)
        cnt = count(lambda kb, _: kb >= cand)
        return jnp.where(cnt >= k_sel, cand_u, t_u)

    thr = lax.fori_loop(0, 32, bit_step, jnp.zeros((1, T), i32)) ^ jnp.int32(INT_MIN)

    n_ge = count(lambda kb, _: kb >= thr)
    tied = jnp.logical_and(n_ge > k_sel, thr != jnp.int32(INT_MIN))

    def tie_limit():
        need = k_sel - count(lambda kb, _: kb > thr)
        n_bits = max(1, (seq - 1).bit_length())

        def idx_step(b, lim):
            bit = lax.shift_left(jnp.int32(1), n_bits - 1 - b)
            probe = lim + bit - 1
            cnt = count(lambda kb, start: jnp.logical_and(kb == thr, key_iota + start <= probe))
            return jnp.where(cnt < need, lim + bit, lim)

        lim = lax.fori_loop(0, n_bits, idx_step, jnp.zeros((1, T), i32))
        return jnp.where(tied, lim, jnp.int32(seq))

    any_tied = jnp.max(jnp.where(tied, 1, 0).astype(i32)) > 0
    idx_lim = lax.cond(any_tied, tie_limit, lambda: jnp.full((1, T), seq, i32))

    q_t = q_ref[0, 0]

    def attn_block(j, state, diagonal):
        start = pl.multiple_of(j * T, T)
        kb = k_ref[0, pl.ds(start, T), :]
        v_t = v_ref[0, j]
        keys = key_ref[pl.ds(start, T), :]
        sel = jnp.logical_or(keys > thr,
                             jnp.logical_and(keys == thr, key_iota + start <= idx_lim))
        if diagonal:
            sel = jnp.logical_and(sel, causal)
        out = []
        for h in range(D_HEADS):
            s = jnp.dot(kb, q_t[h * HEAD_DIM:(h + 1) * HEAD_DIM], preferred_element_type=f32)
            s = s * (HEAD_DIM ** -0.5) + bias_ref[h, jnp.minimum(i - j, N_FAR)]
            s = jnp.where(sel, s, NEG_MASK)
            out.append(_softmax_step(s, v_t, state[h]))
        return tuple(out)

    state = tuple(_softmax_init(HEAD_DIM) for _ in range(D_HEADS))
    state = lax.fori_loop(0, i, lambda j, st: attn_block(j, st, False), state)
    state = attn_block(i, state, True)
    for h in range(D_HEADS):
        _, l, acc = state[h]
        o_ref[0, 0, h * HEAD_DIM:(h + 1) * HEAD_DIM, :] = (acc / l).astype(o_ref.dtype)


def _sparse_attn_call(q_t, k, v_t, iq_t, ik, iw_t, bias):
    bsz, nt, _, _ = q_t.shape
    seq = nt * T
    k_sel = min(D_TOPK_MAX, seq // 4)
    tile = lambda f: pl.BlockSpec((1, 1, f, T), lambda b, i: (b, i, 0, 0))
    whole = lambda shape: pl.BlockSpec((1,) + shape, lambda b, i: (b,) + (0,) * len(shape))
    return pl.pallas_call(
        functools.partial(_sparse_attn_kernel, k_sel=k_sel, seq=seq), name="mixer_d_sparse_attn",
        grid=(bsz, nt),
        in_specs=[tile(D_HEADS * HEAD_DIM), whole((seq, HEAD_DIM)), whole((nt, HEAD_DIM, T)),
                  tile(IDX_HEADS * IDX_DIM), whole((seq, IDX_DIM)), tile(IDX_HEADS),
                  _const_spec(bias.shape)],
        out_specs=tile(D_HEADS * HEAD_DIM),
        out_shape=jax.ShapeDtypeStruct((bsz, nt, D_HEADS * HEAD_DIM, T), bf16),
        scratch_shapes=[pltpu.VMEM((seq, T), i32)],
        compiler_params=_params(2),
    )(q_t, k, v_t, iq_t, ik, iw_t, bias)


def _band_kernel(*refs, hq, hk, has_sink, tb):
    if has_sink:
        sink_ref, q_ref, kp_ref, kc_ref, vp_ref, vc_ref, bias_ref, o_ref = refs
    else:
        q_ref, kp_ref, kc_ref, vp_ref, vc_ref, bias_ref, o_ref, lse_ref = refs
    i = pl.program_id(1)
    rep = hq // hk
    from_prev_tile = lax.broadcasted_iota(i32, (BAND, 2 * BAND), 1) < BAND
    contract_last = (((1,), (1,)), ((), ()))
    for a in range(tb // BAND):
        if a == 0:
            kk = jnp.concatenate([kp_ref[0], kc_ref[0, 0:BAND, :]], axis=0)
            vv = jnp.concatenate([vp_ref[0], vc_ref[0, 0:BAND, :]], axis=0)
        else:
            kk = kc_ref[0, (a - 1) * BAND:(a + 1) * BAND, :]
            vv = vc_ref[0, (a - 1) * BAND:(a + 1) * BAND, :]
        qa = q_ref[0, a * BAND:(a + 1) * BAND, :]
        outs, lses = [], []
        for h in range(hq):
            g = h // rep
            hs = slice(h * HEAD_DIM, (h + 1) * HEAD_DIM)
            gs = slice(g * HEAD_DIM, (g + 1) * HEAD_DIM)
            s = lax.dot_general(qa[:, hs], kk[:, gs], contract_last, preferred_element_type=f32)
            s = s * (HEAD_DIM ** -0.5) + bias_ref[h]
            if a == 0:
                s = jnp.where(jnp.logical_and(from_prev_tile, i == 0), NEG_MASK, s)
            m = jnp.max(s, axis=1, keepdims=True)
            if has_sink:
                m = jnp.maximum(m, sink_ref[h])
            e = jnp.exp(s - m)
            den = jnp.sum(e, axis=1, keepdims=True)
            if has_sink:
                den = den + jnp.exp(sink_ref[h] - m)
            o = jnp.dot(e.astype(bf16), vv[:, gs], preferred_element_type=f32) / den
            outs.append(o)
            if not has_sink:
                lses.append(jnp.broadcast_to(m + jnp.log(den), (BAND, HEAD_DIM)))
        rows = slice(a * BAND, (a + 1) * BAND)
        o_ref[0, rows, :] = jnp.concatenate(outs, axis=1).astype(o_ref.dtype)
        if not has_sink:
            lse_ref[0, rows, :] = jnp.concatenate(lses, axis=1)


def _band_call(name, q, k, v, bias, sinks):
    bn, sn, qw = q.shape
    kw = k.shape[2]
    hq, hk = qw // HEAD_DIM, kw // HEAD_DIM
    tb = min(BAND_TILE, sn)
    per = tb // BAND
    has_sink = sinks is not None
    cur = lambda w: pl.BlockSpec((1, tb, w), lambda b, i: (b, i, 0))
    prev = lambda w: pl.BlockSpec((1, BAND, w), lambda b, i: (b, jnp.maximum(i * per - 1, 0), 0))
    in_specs = [cur(qw), prev(kw), cur(kw), prev(kw), cur(kw), _const_spec(bias.shape)]
    args = [q, k, k, v, v, bias]
    if has_sink:
        in_specs = [pl.BlockSpec(memory_space=pltpu.SMEM)] + in_specs
        args = [sinks.astype(f32)] + args
        out_specs = cur(qw)
        out_shape = jax.ShapeDtypeStruct((bn, sn, qw), bf16)
    else:
        out_specs = [cur(qw), cur(qw)]
        out_shape = [jax.ShapeDtypeStruct((bn, sn, qw), f32)] * 2
    return pl.pallas_call(
        functools.partial(_band_kernel, hq=hq, hk=hk, has_sink=has_sink, tb=tb), name=name,
        grid=(bn, sn // tb), in_specs=in_specs, out_specs=out_specs, out_shape=out_shape,
        compiler_params=_params(2),
    )(*args)


def _merge_kernel(x_ref, shift_ref, scale_ref, gate_ref, oa_ref, ob_ref, oc_refs, lse_refs, od_ref,
                  wg_ref, wb_ref, wo_ref, g_ref, b_ref, o_ref):
    x = x_ref[0]
    u = _modulate(x_ref, shift_ref, scale_ref).astype(bf16)
    lse = [r[0] for r in lse_refs]
    top = jnp.maximum(jnp.maximum(lse[0], lse[1]), lse[2])
    ws = [jnp.exp(v - top) for v in lse]
    o_c = (ws[0] * oc_refs[0][0] + ws[1] * oc_refs[1][0] + ws[2] * oc_refs[2][0]) / (ws[0] + ws[1] + ws[2])
    contract_first = (((0,), (0,)), ((), ()))
    branch = (
        lax.dot_general(oa_ref[0, 0], wb_ref[0], contract_first, preferred_element_type=f32),
        jnp.dot(ob_ref[0], wb_ref[1], preferred_element_type=f32),
        jnp.dot(o_c.astype(bf16), wb_ref[2], preferred_element_type=f32),
        lax.dot_general(od_ref[0, 0], wb_ref[3], contract_first, preferred_element_type=f32),
    )
    merged = None
    for n in range(N_BRANCH):
        logits = jnp.dot(u, wg_ref[:, n * D_MODEL:(n + 1) * D_MODEL], preferred_element_type=f32)
        term = jax.nn.sigmoid(logits) * branch[n]
        merged = term if merged is None else merged + term
    y = jnp.dot(merged.astype(bf16), wo_ref[...], preferred_element_type=f32)
    z = DEEPNORM_ALPHA * x + (1.0 + gate_ref[0]) * y
    o_ref[0] = _layer_norm(z, g_ref[...], b_ref[...])


def _merge_call(x, shift, scale, gate, o_a, o_b, o_c, lse_c, o_d, w_gate, w_branch, w_out, ln_g, ln_b):
    bsz, seq, d = x.shape
    nt = seq // T
    tok = lambda w: pl.BlockSpec((1, T, w), lambda b, i: (b, i, 0))
    feat = lambda w: pl.BlockSpec((1, 1, w, T), lambda b, i: (b, i, 0, 0))
    mod = pl.BlockSpec((1, 1, d), lambda b, i: (b, 0, 0))

    def kernel(*refs):
        (x_ref, s_ref, c_ref, g_ref, oa, ob, c0, c1, c2, l0, l1, l2, od, wg, wb, wo, lg, lb, o) = refs
        _merge_kernel(x_ref, s_ref, c_ref, g_ref, oa, ob, (c0, c1, c2), (l0, l1, l2), od, wg, wb, wo, lg, lb, o)

    return pl.pallas_call(
        kernel, name="merge_outproj_ln1",
        grid=(bsz, nt),
        in_specs=[tok(d), mod, mod, mod, feat(BRANCH_WIDTH), tok(BRANCH_WIDTH)]
                 + [tok(BRANCH_WIDTH)] * 6 + [feat(BRANCH_WIDTH),
                 _const_spec(w_gate.shape), _const_spec(w_branch.shape), _const_spec(w_out.shape),
                 _const_spec((1, d)), _const_spec((1, d))],
        out_specs=tok(d),
        out_shape=jax.ShapeDtypeStruct((bsz, seq, d), f32),
        compiler_params=_params(2),
    )(x, shift, scale, gate, o_a, o_b, *o_c, *lse_c, o_d, w_gate, w_branch, w_out,
      ln_g.reshape(1, d), ln_b.reshape(1, d))


ROUTER_TILE = 512


def _router_kernel(x_ref, shift_ref, scale_ref, wr_ref, br_ref, o_ref):
    v = _modulate(x_ref, shift_ref, scale_ref)
    logits = lax.dot_general(wr_ref[...], v, (((1,), (1,)), ((), ())), preferred_element_type=f32,
                             precision=lax.Precision.HIGHEST)
    e = jnp.exp(logits - jnp.max(logits, axis=0, keepdims=True))
    probs = e / jnp.sum(e, axis=0, keepdims=True)
    score = probs + br_ref[...]
    p = [probs[n:n + 1] for n in range(N_EXPERTS)]
    s = [score[n:n + 1] for n in range(N_EXPERTS)]
    best_g = None
    for g in range(N_GROUPS):
        a, b, c, d = s[g * EXPERTS_PER_GROUP:(g + 1) * EXPERTS_PER_GROUP]
        hi1, lo1, hi2, lo2 = jnp.maximum(a, b), jnp.minimum(a, b), jnp.maximum(c, d), jnp.minimum(c, d)
        gs = jnp.maximum(hi1, hi2) + jnp.maximum(jnp.minimum(hi1, hi2), jnp.maximum(lo1, lo2))
        if best_g is None:
            best_g, grp = gs, jnp.zeros_like(gs, dtype=i32)
        else:
            better = gs > best_g
            best_g = jnp.where(better, gs, best_g)
            grp = jnp.where(better, g, grp)
    neg_inf = jnp.float32(-jnp.inf)
    cand = [jnp.where(grp == n // EXPERTS_PER_GROUP, s[n], neg_inf) for n in range(N_EXPERTS)]

    def first_argmax(vals):
        best, idx = vals[0], jnp.zeros_like(grp)
        for n in range(1, N_EXPERTS):
            better = vals[n] > best
            best = jnp.where(better, vals[n], best)
            idx = jnp.where(better, n, idx)
        return idx

    i1 = first_argmax(cand)
    i2 = first_argmax([jnp.where(i1 == n, neg_inf, cand[n]) for n in range(N_EXPERTS)])
    w1 = sum(jnp.where(i1 == n, p[n], 0.0) for n in range(N_EXPERTS))
    w2 = sum(jnp.where(i2 == n, p[n], 0.0) for n in range(N_EXPERTS))
    tot = w1 + w2
    rows = [jnp.where(i1 == n, w1 / tot, jnp.where(i2 == n, w2 / tot, 0.0)) for n in range(N_EXPERTS)]
    o_ref[0] = jnp.concatenate(rows, axis=0)


def _router_call(x, shift, scale, w_router, b_router):
    bsz, seq, d = x.shape
    rt = min(ROUTER_TILE, seq)
    mod = pl.BlockSpec((1, 1, d), lambda b, i: (b, 0, 0))
    return pl.pallas_call(
        _router_kernel, name="moe_router",
        grid=(bsz, seq // rt),
        in_specs=[pl.BlockSpec((1, rt, d), lambda b, i: (b, i, 0)), mod, mod,
                  _const_spec((N_EXPERTS, d)), _const_spec((N_EXPERTS, 1))],
        out_specs=pl.BlockSpec((1, N_EXPERTS, rt), lambda b, i: (b, 0, i)),
        out_shape=jax.ShapeDtypeStruct((bsz, N_EXPERTS, seq), f32),
        compiler_params=_params(2),
    )(x, shift, scale, w_router.T.astype(f32), b_router.astype(f32).reshape(N_EXPERTS, 1))


def _moe_kernel(x_ref, shift_ref, scale_ref, gate_ref, wt_ref, wg_ref, wu_ref, wd_ref, g_ref, b_ref, o_ref):
    x = x_ref[0]
    v = _modulate(x_ref, shift_ref, scale_ref).astype(bf16)
    wt = wt_ref[0]
    parts = []
    for n in range(N_EXPERTS):
        cols = slice(n * D_FF_EXPERT, (n + 1) * D_FF_EXPERT)
        hg = jnp.dot(v, wg_ref[:, cols], preferred_element_type=f32)
        hu = jnp.dot(v, wu_ref[:, cols], preferred_element_type=f32)
        w_n = wt[:, n:n + 1]
        act = hg * jax.nn.sigmoid(hg) * hu
        parts.append(jnp.where(w_n != 0.0, act * w_n, 0.0).astype(bf16))
    h = jnp.concatenate(parts, axis=1)
    y = jnp.dot(h, wd_ref[...], preferred_element_type=f32)
    z = DEEPNORM_ALPHA * x + (1.0 + gate_ref[0]) * y
    o_ref[0] = _layer_norm(z, g_ref[...], b_ref[...])


def _moe_call(x, shift, scale, gate, wt, w_gate, w_up, w_down, ln_g, ln_b):
    bsz, seq, d = x.shape
    tok = lambda w: pl.BlockSpec((1, T, w), lambda b, i: (b, i, 0))
    mod = pl.BlockSpec((1, 1, d), lambda b, i: (b, 0, 0))
    return pl.pallas_call(
        _moe_kernel, name="moe_experts_ln2",
        grid=(bsz, seq // T),
        in_specs=[tok(d), mod, mod, mod, tok(N_EXPERTS),
                  _const_spec(w_gate.shape), _const_spec(w_up.shape), _const_spec(w_down.shape),
                  _const_spec((1, d)), _const_spec((1, d))],
        out_specs=tok(d),
        out_shape=jax.ShapeDtypeStruct((bsz, seq, d), f32),
        compiler_params=_params(2),
    )(x, shift, scale, gate, wt, w_gate, w_up, w_down, ln_g.reshape(1, d), ln_b.reshape(1, d))


def _t5_bucket(n):
    max_exact = REL_BUCKETS // 2
    nf = jnp.maximum(n, 1).astype(f32)
    log_ratio = jnp.log(nf / max_exact) / math.log(REL_MAX_DIST / max_exact)
    large = max_exact + (log_ratio * (REL_BUCKETS - max_exact)).astype(i32)
    large = jnp.minimum(large, REL_BUCKETS - 1)
    return jnp.where(n < max_exact, n, large)


def _bias_tables(rel_table, seq):
    n_dist = max(seq, REL_MAX_DIST + 1)
    by_dist = rel_table.astype(f32)[_t5_bucket(jnp.arange(n_dist, dtype=i32))]
    delta = np.arange(N_FAR + 1)[:, None, None]
    dist = delta * T + np.arange(T)[None, None, :] - np.arange(T)[None, :, None]
    dense = by_dist[np.clip(dist, 0, n_dist - 1)]
    dense = jnp.moveaxis(dense, -1, 0)
    bias_a = dense[A_BIAS0:A_BIAS0 + A_HEADS]
    bias_d = dense[D_BIAS0:D_BIAS0 + D_HEADS]
    band_dist = np.arange(BAND)[:, None] + BAND - np.arange(2 * BAND)[None, :]

    def band(h0, nh, max_dist, dilation):
        valid = (band_dist >= 0) & (band_dist <= max_dist)
        vals = by_dist[np.clip(band_dist, 0, BAND) * dilation][..., h0:h0 + nh]
        return jnp.moveaxis(jnp.where(valid[..., None], vals, NEG_MASK), -1, 0)

    bias_b = band(B_BIAS0, B_QHEADS, B_WINDOW - 1, 1)
    bias_c = [band(C_BIAS0 + g * C_HEADS, C_HEADS, window // dil, dil)
              for g, (window, dil) in enumerate(C_PATTERNS)]
    return bias_a, bias_b, bias_c, bias_d


def _split_cols(w):
    parts, off = {}, 0
    for name, width in IN_SPLITS:
        parts[name] = w[:, off:off + width]
        off += width
    return parts


def _dilate(a, r):
    if r == 1:
        return a
    b, s, f = a.shape
    return a.reshape(b, s // r, r, f).transpose(0, 2, 1, 3).reshape(b * r, s // r, f)


def _undilate(a, r, bsz):
    if r == 1:
        return a
    _, sn, f = a.shape
    return a.reshape(bsz, r, sn, f).transpose(0, 2, 1, 3).reshape(bsz, sn * r, f)


def _layer(l, x, mod, tables, w_router, b_router, w_in, lam_q, lam_k, subln_g, sinks, w_branch, w_out,
           ln1_g, ln1_b, w_exp_gate, w_exp_up, w_exp_down, ln2_g, ln2_b):
    bsz, seq, d = x.shape
    shift_a, scale_a, gate_a, shift_f, scale_f, gate_f = mod
    bias_a, bias_b, bias_c, bias_d = tables
    w = _split_cols(w_in.astype(bf16))
    hw = C_HEADS * HEAD_DIM

    aq, ak, av = _proj_call("proj_a", x, shift_a, scale_a, [w['a_q'], w['a_k'], w['a_v']],
                            [('feat',), ('tokh', 2 * A_HEADS, A_QK), ('feat',)], [bf16] * 3)
    o_a = _diff_attn_call(aq, ak, av, bias_a, lam_q, lam_k, subln_g, l)

    bq, bk, bv = _proj_call("proj_b", x, shift_a, scale_a, [w['b_q'], w['b_k'], w['b_v']],
                            [('tok',)] * 3, [bf16] * 3)
    o_b = _band_call("mixer_b_window_attn", bq, bk, bv, bias_b, sinks)

    c_cols = [w[n][:, g * hw:(g + 1) * hw] for g in range(len(C_PATTERNS)) for n in ('c_q', 'c_k', 'c_v')]
    c_proj = _proj_call("proj_c", x, shift_a, scale_a, c_cols, [('tok',)] * 9, [bf16] * 9)
    o_c, lse_c = [], []
    for g, (_, dil) in enumerate(C_PATTERNS):
        qg, kg, vg = (_dilate(a, dil) for a in c_proj[3 * g:3 * g + 3])
        og, lg = _band_call(f"mixer_c_dilated_attn_{g}", qg, kg, vg, bias_c[g], None)
        o_c.append(_undilate(og, dil, bsz))
        lse_c.append(_undilate(lg, dil, bsz))

    dq, dk, dv, iq, ik, iw = _proj_call(
        "proj_d", x, shift_a, scale_a, [w['d_q'], w['d_k'], w['d_v'], w['i_q'], w['i_k'], w['i_w']],
        [('feat',), ('tok',), ('feat',), ('feat',), ('tok',), ('feat',)], [bf16] * 5 + [f32])
    o_d = _sparse_attn_call(dq, dk, dv, iq, ik, iw, bias_d)

    x1 = _merge_call(x, shift_a, scale_a, gate_a, o_a, o_b, o_c, lse_c, o_d, w['gate'],
                     w_branch.astype(bf16), w_out.astype(bf16), ln1_g, ln1_b)

    wt = _router_call(x1, shift_f, scale_f, w_router, b_router)
    wt = jnp.swapaxes(wt, 1, 2)
    n_ff = N_EXPERTS * D_FF_EXPERT
    wg_all = jnp.moveaxis(w_exp_gate, 0, 1).reshape(d, n_ff).astype(bf16)
    wu_all = jnp.moveaxis(w_exp_up, 0, 1).reshape(d, n_ff).astype(bf16)
    wd_all = w_exp_down.reshape(n_ff, d).astype(bf16)
    return _moe_call(x1, shift_f, scale_f, gate_f, wt, wg_all, wu_all, wd_all, ln2_g, ln2_b)


def kernel(x, c, rel_table, w_router, b_router, w_ada, b_ada, w_in, a_lambda_q, a_lambda_k, a_subln_g,
           b_sinks, w_branch, w_out, ln1_g, ln1_b, w_exp_gate, w_exp_up, w_exp_down, ln2_g, ln2_b):
    bsz, seq, d = x.shape
    assert d == D_MODEL and seq % (BAND * C_PATTERNS[-1][1]) == 0 and seq % BAND_TILE == 0
    mod = _ada_call(c, w_ada, b_ada)
    tables = _bias_tables(rel_table, seq)
    for l in range(DEPTH):
        mod_l = tuple(m.reshape(bsz, 1, d) for m in jnp.split(mod[l], 6, axis=-1))
        x = _layer(l, x, mod_l, tables, w_router, b_router, w_in[l], a_lambda_q[l], a_lambda_k[l],
                   a_subln_g[l], b_sinks[l], w_branch[l], w_out[l], ln1_g[l], ln1_b[l],
                   w_exp_gate[l], w_exp_up[l], w_exp_down[l], ln2_g[l], ln2_b[l])
    return x
```

```python
import functools
import math

import numpy as np
import jax
import jax.numpy as jnp
from jax import lax
from jax.experimental import pallas as pl
from jax.experimental.pallas import tpu as pltpu

f32 = jnp.float32
bf16 = jnp.bfloat16
i32 = jnp.int32

D_MODEL = 1024
DEPTH = 2
HEAD_DIM = 64
A_HEADS = 4
A_QK = 32
A_V = 2 * A_QK
B_QHEADS = 4
B_KVHEADS = 2
B_WINDOW = 128
C_PATTERNS = ((128, 1), (512, 4), (2048, 16))
C_HEADS = 4
D_HEADS = 4
D_TOPK_MAX = 256
IDX_HEADS = 8
IDX_DIM = 32
REL_BUCKETS = 32
REL_MAX_DIST = 2048
A_BIAS0 = 0
B_BIAS0 = A_BIAS0 + A_HEADS
C_BIAS0 = B_BIAS0 + B_QHEADS
D_BIAS0 = C_BIAS0 + len(C_PATTERNS) * C_HEADS
N_BIAS_HEADS = D_BIAS0 + D_HEADS
N_BRANCH = 4
BRANCH_WIDTH = 256
N_EXPERTS = 16
N_GROUPS = 4
EXPERTS_PER_GROUP = N_EXPERTS // N_GROUPS
D_FF_EXPERT = 256
DEEPNORM_ALPHA = (2 * DEPTH) ** 0.25
LN_EPS = 1e-5

IN_SPLITS = (
    ('a_q', A_HEADS * 2 * A_QK), ('a_k', A_HEADS * 2 * A_QK), ('a_v', A_HEADS * A_V),
    ('b_q', B_QHEADS * HEAD_DIM), ('b_k', B_KVHEADS * HEAD_DIM), ('b_v', B_KVHEADS * HEAD_DIM),
    ('c_q', len(C_PATTERNS) * C_HEADS * HEAD_DIM), ('c_k', len(C_PATTERNS) * C_HEADS * HEAD_DIM),
    ('c_v', len(C_PATTERNS) * C_HEADS * HEAD_DIM),
    ('d_q', D_HEADS * HEAD_DIM), ('d_k', HEAD_DIM), ('d_v', HEAD_DIM),
    ('i_q', IDX_HEADS * IDX_DIM), ('i_k', IDX_DIM), ('i_w', IDX_HEADS),
    ('gate', N_BRANCH * D_MODEL),
)

LANES = 128
SUBLANES = 8
T = 256
BAND = 128
BAND_TILE = 512
VMEM_LIMIT = 56 * 1024 * 1024
N_FAR = -(-(REL_MAX_DIST + T - 1) // T)

KEY_BLOCKS = 4
LOG2E = math.log2(math.e)

NEG_INIT = -1e30
NEG_MASK = -2e30
INT_MIN = -2 ** 31


def _params(n_axes):
    return pltpu.CompilerParams(dimension_semantics=("arbitrary",) * n_axes,
                                vmem_limit_bytes=VMEM_LIMIT)


def _const_spec(shape):
    nd = len(shape)
    return pl.BlockSpec(shape, lambda *_: (0,) * nd, pipeline_mode=pl.Buffered(1))


def _modulate(x_ref, shift_ref, scale_ref):
    return x_ref[0] * (1.0 + scale_ref[0]) + shift_ref[0]


def _layer_norm(z, g, b):
    mu = jnp.mean(z, axis=-1, keepdims=True)
    zc = z - mu
    var = jnp.mean(zc * zc, axis=-1, keepdims=True)
    return zc * lax.rsqrt(var + LN_EPS) * g + b


def _ada_kernel(c_ref, w_ref, b_ref, o_ref):
    c = c_ref[...]
    cond = c * jax.nn.sigmoid(c)
    o_ref[0] = jnp.dot(cond, w_ref[0], preferred_element_type=f32,
                       precision=lax.Precision.HIGHEST) + b_ref[0]


def _ada_call(c, w_ada, b_ada):
    depth, d, n = w_ada.shape
    bsz = c.shape[0]
    rows = -(-bsz // SUBLANES) * SUBLANES
    c_pad = jnp.zeros((rows, d), f32).at[:bsz].set(c)
    tn = 1536
    out = pl.pallas_call(
        _ada_kernel, name="ada_mod",
        grid=(depth, n // tn),
        in_specs=[pl.BlockSpec((rows, d), lambda l, j: (0, 0)),
                  pl.BlockSpec((1, d, tn), lambda l, j: (l, 0, j)),
                  pl.BlockSpec((1, 1, tn), lambda l, j: (l, 0, j))],
        out_specs=pl.BlockSpec((1, rows, tn), lambda l, j: (l, 0, j)),
        out_shape=jax.ShapeDtypeStruct((depth, rows, n), f32),
        compiler_params=_params(2),
    )(c_pad, w_ada, b_ada.reshape(depth, 1, n))
    return out[:, :bsz]


def _proj_kernel(*refs, kinds):
    n = len(kinds)
    x_ref, shift_ref, scale_ref = refs[:3]
    w_refs = refs[3:3 + n]
    o_refs = refs[3 + n:]
    u = _modulate(x_ref, shift_ref, scale_ref).astype(bf16)
    for kind, w_ref, o_ref in zip(kinds, w_refs, o_refs):
        if kind[0] == 'feat':
            r = lax.dot_general(w_ref[...], u, (((1,), (1,)), ((), ())), preferred_element_type=f32)
            if len(kind) > 1:
                r = r * kind[1]
            o_ref[0, 0] = r.astype(o_ref.dtype)
        elif kind[0] == 'tok':
            r = jnp.dot(u, w_ref[...], preferred_element_type=f32)
            o_ref[0] = r.astype(o_ref.dtype)
        else:
            _, heads, hd = kind
            r = jnp.dot(u, w_ref[...], preferred_element_type=f32)
            for h in range(heads):
                o_ref[0, h] = r[:, h * hd:(h + 1) * hd].astype(o_ref.dtype)


def _proj_call(name, x, shift, scale, w_cols, kinds, dtypes):
    bsz, seq, d = x.shape
    nt = seq // T
    weights, w_specs, out_specs, out_shapes = [], [], [], []
    for w, kind, dt in zip(w_cols, kinds, dtypes):
        feat = w.shape[1]
        if kind[0] == 'feat':
            weights.append(w.T)
            w_specs.append(_const_spec((feat, d)))
            out_specs.append(pl.BlockSpec((1, 1, feat, T), lambda b, i: (b, i, 0, 0)))
            out_shapes.append(jax.ShapeDtypeStruct((bsz, nt, feat, T), dt))
        elif kind[0] == 'tok':
            weights.append(w)
            w_specs.append(_const_spec((d, feat)))
            out_specs.append(pl.BlockSpec((1, T, feat), lambda b, i: (b, i, 0)))
            out_shapes.append(jax.ShapeDtypeStruct((bsz, seq, feat), dt))
        else:
            _, heads, hd = kind
            weights.append(w)
            w_specs.append(_const_spec((d, feat)))
            out_specs.append(pl.BlockSpec((1, heads, T, hd), lambda b, i: (b, 0, i, 0)))
            out_shapes.append(jax.ShapeDtypeStruct((bsz, heads, seq, hd), dt))
    mod_spec = pl.BlockSpec((1, 1, d), lambda b, i: (b, 0, 0))
    return pl.pallas_call(
        functools.partial(_proj_kernel, kinds=tuple(kinds)), name=name,
        grid=(bsz, nt),
        in_specs=[pl.BlockSpec((1, T, d), lambda b, i: (b, i, 0)), mod_spec, mod_spec] + w_specs,
        out_specs=out_specs, out_shape=out_shapes,
        compiler_params=_params(2),
    )(x, shift, scale, *weights)


def _softmax_step(s, v_t, state):
    mx, l, acc = state
    mx_new = jnp.maximum(mx, jnp.max(s, axis=0, keepdims=True))
    alpha = jnp.exp2(mx - mx_new)
    p = jnp.exp2(s - mx_new)
    l = alpha * l + jnp.sum(p, axis=0, keepdims=True)
    acc = alpha * acc + jnp.dot(v_t, p.astype(bf16), preferred_element_type=f32)
    return mx_new, l, acc


def _softmax_init(dv):
    return (jnp.full((1, T), NEG_INIT, f32), jnp.zeros((1, T), f32), jnp.zeros((dv, T), f32))


def _causal_tile():
    key = lax.broadcasted_iota(i32, (T, T), 0)
    qry = lax.broadcasted_iota(i32, (T, T), 1)
    return key <= qry


def _sweep_key_tiles(i, tile_logits, tile_values, state):
    n_full = i // KEY_BLOCKS

    def run(first, n, st, ends_on_diagonal):
        logits = [tile_logits(first + t, ends_on_diagonal and t == n - 1) for t in range(n)]
        for t in range(n):
            v_t = tile_values(first + t)
            st = tuple(_softmax_step(s, v_t, st_h) for s, st_h in zip(logits[t], st))
        return st

    state = lax.fori_loop(0, n_full, lambda js, st: run(js * KEY_BLOCKS, KEY_BLOCKS, st, False), state)
    tail = [functools.partial(lambda n, st: run(n_full * KEY_BLOCKS, n, st, True), n)
            for n in range(1, KEY_BLOCKS + 1)]
    return lax.switch(i - n_full * KEY_BLOCKS, tail, state)


def _diff_attn_kernel(lq_ref, lk_ref, q_ref, k_ref, v_ref, bias_ref, g_ref, o_ref, *, lam_init):
    i = pl.program_id(2)
    q_t = q_ref[0, 0]
    q_parts = (q_t[:A_QK], q_t[A_QK:])
    causal = _causal_tile()

    def tile_logits(j, diagonal):
        start = pl.multiple_of(j * T, T)
        bias = bias_ref[0, jnp.minimum(i - j, N_FAR)]
        out = []
        for m in range(2):
            kb = k_ref[0, m, pl.ds(start, T), :]
            s = jnp.dot(kb, q_parts[m], preferred_element_type=f32) + bias
            out.append(jnp.where(causal, s, NEG_MASK) if diagonal else s)
        return out

    state = _sweep_key_tiles(i, tile_logits, lambda j: v_ref[0, j],
                             (_softmax_init(A_V), _softmax_init(A_V)))
    (_, l0, acc0), (_, l1, acc1) = state

    lq = lq_ref[...]
    lk = lk_ref[...]
    lam = (jnp.exp(jnp.sum(lq[0:1] * lk[0:1], axis=1, keepdims=True))
           - jnp.exp(jnp.sum(lq[1:2] * lk[1:2], axis=1, keepdims=True)) + lam_init)
    o = acc0 / l0 - lam * (acc1 / l1)
    ms = jnp.mean(o * o, axis=0, keepdims=True)
    o = o * lax.rsqrt(ms + LN_EPS) * g_ref[...] * (1.0 - lam_init)
    o_ref[0, 0] = o.astype(o_ref.dtype)


def _diff_attn_call(q_t, k_h, v_t, bias, lam_q, lam_k, subln_g, layer_idx):
    bsz, nt, _, _ = q_t.shape
    seq = nt * T
    lam_init = 0.8 - 0.6 * math.exp(-0.3 * layer_idx)
    g = jnp.broadcast_to(subln_g.astype(f32)[:, None], (A_V, T))
    nb = bias.shape[1]
    return pl.pallas_call(
        functools.partial(_diff_attn_kernel, lam_init=lam_init), name="mixer_a_diff_attn",
        grid=(bsz, A_HEADS, nt),
        in_specs=[_const_spec((2, A_QK)), _const_spec((2, A_QK)),
                  pl.BlockSpec((1, 1, 2 * A_QK, T), lambda b, h, i: (b, i, h, 0)),
                  pl.BlockSpec((1, 2, seq, A_QK), lambda b, h, i: (b, h, 0, 0)),
                  pl.BlockSpec((1, nt, A_V, T), lambda b, h, i: (b, 0, h, 0)),
                  pl.BlockSpec((1, nb, T, T), lambda b, h, i: (h, 0, 0, 0)),
                  _const_spec((A_V, T))],
        out_specs=pl.BlockSpec((1, 1, A_V, T), lambda b, h, i: (b, i, h, 0)),
        out_shape=jax.ShapeDtypeStruct((bsz, nt, A_HEADS * A_V, T), bf16),
        compiler_params=_params(3),
    )(lam_q.astype(f32), lam_k.astype(f32), q_t, k_h, v_t, bias, g)


def _sparse_attn_kernel(q_ref, k_ref, v_ref, iq_ref, ik_ref, iw_ref, bias_ref, o_ref, key_ref,
                        *, k_sel, seq):
    i = pl.program_id(1)
    causal = _causal_tile()
    key_iota = lax.broadcasted_iota(i32, (T, T), 0)
    iq_t = iq_ref[0, 0]
    iw_t = iw_ref[0, 0]

    def score_block(j, diagonal):
        start = pl.multiple_of(j * T, T)
        ikb = ik_ref[0, pl.ds(start, T), :]
        sc = jnp.zeros((T, T), f32)
        for h in range(IDX_HEADS):
            r = jnp.dot(ikb, iq_t[h * IDX_DIM:(h + 1) * IDX_DIM], preferred_element_type=f32)
            sc = sc + jnp.maximum(r, 0.0) * iw_t[h:h + 1]
        bits = pltpu.bitcast(sc, i32)
        key = bits ^ ((bits >> 31) & jnp.int32(0x7FFFFFFF))
        if diagonal:
            key = jnp.where(causal, key, jnp.int32(INT_MIN))
        key_ref[pl.ds(start, T), :] = key

    def _score_body(j, carry):
        score_block(j, False)
        return carry

    lax.fori_loop(0, i, _score_body, 0)
    score_block(i, True)

    def count(pred):
        def body(j, acc):
            start = pl.multiple_of(j * T, T)
            ind = jnp.where(pred(key_ref[pl.ds(start, T), :], start), 1, 0).astype(i32)
            return acc + ind.reshape(T // SUBLANES, SUBLANES, T).sum(axis=0)
        acc = lax.fori_loop(0, i + 1, body, jnp.zeros((SUBLANES, T), i32))
        return acc.sum(axis=0, keepdims=True)

    def bit_step(carry):
        b, t_u, n_ge, _ = carry
        cand_u = t_u | lax.shift_left(jnp.int32(1), 31 - b)
        cand = cand_u ^ jnp.int32(INT_MIN)
        cnt = count(lambda kb, _: kb >= cand)
        take = cnt >= k_sel
        n_ge = jnp.where(take, cnt, n_ge)
        return b + 1, jnp.where(take, cand_u, t_u), n_ge, jnp.max(n_ge) > k_sel

    n_valid = i * T + lax.broadcasted_iota(i32, (1, T), 1) + 1
    _, t_u, n_ge, _ = lax.while_loop(
        lambda c: jnp.logical_and(c[0] < 32, c[3]), bit_step,
        (jnp.int32(0), jnp.zeros((1, T), i32), n_valid, (i + 1) * T > k_sel))
    thr = jnp.maximum(t_u ^ jnp.int32(INT_MIN), jnp.int32(INT_MIN + 1))

    tied = n_ge > k_sel

    def tie_limit():
        need = k_sel - count(lambda kb, _: kb > thr)
        n_bits = max(1, (seq - 1).bit_length())

        def idx_step(b, lim):
            bit = lax.shift_left(jnp.int32(1), n_bits - 1 - b)
            probe = lim + bit - 1
            cnt = count(lambda kb, start: jnp.logical_and(kb == thr, key_iota + start <= probe))
            return jnp.where(cnt < need, lim + bit, lim)

        lim = lax.fori_loop(0, n_bits, idx_step, jnp.zeros((1, T), i32))
        return jnp.where(tied, lim, jnp.int32(seq))

    any_tied = jnp.max(jnp.where(tied, 1, 0).astype(i32)) > 0
    idx_lim = lax.cond(any_tied, tie_limit, lambda: jnp.full((1, T), seq, i32))

    q_t = q_ref[0, 0]

    def tile_logits(j, diagonal):
        del diagonal
        start = pl.multiple_of(j * T, T)
        kb = k_ref[0, pl.ds(start, T), :]
        keys = key_ref[pl.ds(start, T), :]
        drop = jnp.logical_or(keys < thr, jnp.logical_and(keys == thr, key_iota + start > idx_lim))
        out = []
        for h in range(D_HEADS):
            s = jnp.dot(kb, q_t[h * HEAD_DIM:(h + 1) * HEAD_DIM], preferred_element_type=f32)
            out.append(jnp.where(drop, NEG_MASK, s + bias_ref[h, jnp.minimum(i - j, N_FAR)]))
        return out

    state = _sweep_key_tiles(i, tile_logits, lambda j: v_ref[0, j],
                             tuple(_softmax_init(HEAD_DIM) for _ in range(D_HEADS)))
    for h in range(D_HEADS):
        _, l, acc = state[h]
        o_ref[0, 0, h * HEAD_DIM:(h + 1) * HEAD_DIM, :] = (acc / l).astype(o_ref.dtype)


def _sparse_attn_call(q_t, k, v_t, iq_t, ik, iw_t, bias):
    bsz, nt, _, _ = q_t.shape
    seq = nt * T
    k_sel = min(D_TOPK_MAX, seq // 4)
    tile = lambda f: pl.BlockSpec((1, 1, f, T), lambda b, i: (b, i, 0, 0))
    whole = lambda shape: pl.BlockSpec((1,) + shape, lambda b, i: (b,) + (0,) * len(shape))
    return pl.pallas_call(
        functools.partial(_sparse_attn_kernel, k_sel=k_sel, seq=seq), name="mixer_d_sparse_attn",
        grid=(bsz, nt),
        in_specs=[tile(D_HEADS * HEAD_DIM), whole((seq, HEAD_DIM)), whole((nt, HEAD_DIM, T)),
                  tile(IDX_HEADS * IDX_DIM), whole((seq, IDX_DIM)), tile(IDX_HEADS),
                  _const_spec(bias.shape)],
        out_specs=tile(D_HEADS * HEAD_DIM),
        out_shape=jax.ShapeDtypeStruct((bsz, nt, D_HEADS * HEAD_DIM, T), bf16),
        scratch_shapes=[pltpu.VMEM((seq, T), i32)],
        compiler_params=_params(2),
    )(q_t, k, v_t, iq_t, ik, iw_t, bias)


def _band_kernel(*refs, hq, hk, has_sink, tb):
    if has_sink:
        sink_ref, q_ref, kp_ref, kc_ref, vp_ref, vc_ref, bias_ref, o_ref = refs
    else:
        q_ref, kp_ref, kc_ref, vp_ref, vc_ref, bias_ref, o_ref, lse_ref = refs
    i = pl.program_id(1)
    rep = hq // hk
    from_prev_tile = lax.broadcasted_iota(i32, (BAND, 2 * BAND), 1) < BAND
    contract_last = (((1,), (1,)), ((), ()))
    for a in range(tb // BAND):
        if a == 0:
            kk = jnp.concatenate([kp_ref[0], kc_ref[0, 0:BAND, :]], axis=0)
            vv = jnp.concatenate([vp_ref[0], vc_ref[0, 0:BAND, :]], axis=0)
        else:
            kk = kc_ref[0, (a - 1) * BAND:(a + 1) * BAND, :]
            vv = vc_ref[0, (a - 1) * BAND:(a + 1) * BAND, :]
        qa = q_ref[0, a * BAND:(a + 1) * BAND, :]
        outs, lses = [], []
        for h in range(hq):
            g = h // rep
            hs = slice(h * HEAD_DIM, (h + 1) * HEAD_DIM)
            gs = slice(g * HEAD_DIM, (g + 1) * HEAD_DIM)
            s = lax.dot_general(qa[:, hs], kk[:, gs], contract_last, preferred_element_type=f32)
            s = s * (HEAD_DIM ** -0.5) + bias_ref[h]
            if a == 0:
                s = jnp.where(jnp.logical_and(from_prev_tile, i == 0), NEG_MASK, s)
            m = jnp.max(s, axis=1, keepdims=True)
            if has_sink:
                m = jnp.maximum(m, sink_ref[h])
            e = jnp.exp(s - m)
            den = jnp.sum(e, axis=1, keepdims=True)
            if has_sink:
                den = den + jnp.exp(sink_ref[h] - m)
            o = jnp.dot(e.astype(bf16), vv[:, gs], preferred_element_type=f32) / den
            outs.append(o)
            if not has_sink:
                lses.append(jnp.broadcast_to(m + jnp.log(den), (BAND, HEAD_DIM)))
        rows = slice(a * BAND, (a + 1) * BAND)
        o_ref[0, rows, :] = jnp.concatenate(outs, axis=1).astype(o_ref.dtype)
        if not has_sink:
            lse_ref[0, rows, :] = jnp.concatenate(lses, axis=1)


def _band_call(name, q, k, v, bias, sinks):
    bn, sn, qw = q.shape
    kw = k.shape[2]
    hq, hk = qw // HEAD_DIM, kw // HEAD_DIM
    tb = min(BAND_TILE, sn)
    per = tb // BAND
    has_sink = sinks is not None
    cur = lambda w: pl.BlockSpec((1, tb, w), lambda b, i: (b, i, 0))
    prev = lambda w: pl.BlockSpec((1, BAND, w), lambda b, i: (b, jnp.maximum(i * per - 1, 0), 0))
    in_specs = [cur(qw), prev(kw), cur(kw), prev(kw), cur(kw), _const_spec(bias.shape)]
    args = [q, k, k, v, v, bias]
    if has_sink:
        in_specs = [pl.BlockSpec(memory_space=pltpu.SMEM)] + in_specs
        args = [sinks.astype(f32)] + args
        out_specs = cur(qw)
        out_shape = jax.ShapeDtypeStruct((bn, sn, qw), bf16)
    else:
        out_specs = [cur(qw), cur(qw)]
        out_shape = [jax.ShapeDtypeStruct((bn, sn, qw), f32)] * 2
    return pl.pallas_call(
        functools.partial(_band_kernel, hq=hq, hk=hk, has_sink=has_sink, tb=tb), name=name,
        grid=(bn, sn // tb), in_specs=in_specs, out_specs=out_specs, out_shape=out_shape,
        compiler_params=_params(2),
    )(*args)


def _merge_kernel(x_ref, shift_ref, scale_ref, gate_ref, oa_ref, ob_ref, oc_refs, lse_refs, od_ref,
                  wg_ref, wb_ref, wo_ref, g_ref, b_ref, o_ref):
    x = x_ref[0]
    u = _modulate(x_ref, shift_ref, scale_ref).astype(bf16)
    lse = [r[0] for r in lse_refs]
    top = jnp.maximum(jnp.maximum(lse[0], lse[1]), lse[2])
    ws = [jnp.exp(v - top) for v in lse]
    o_c = (ws[0] * oc_refs[0][0] + ws[1] * oc_refs[1][0] + ws[2] * oc_refs[2][0]) / (ws[0] + ws[1] + ws[2])
    contract_first = (((0,), (0,)), ((), ()))
    branch = (
        lax.dot_general(oa_ref[0, 0], wb_ref[0], contract_first, preferred_element_type=f32),
        jnp.dot(ob_ref[0], wb_ref[1], preferred_element_type=f32),
        jnp.dot(o_c.astype(bf16), wb_ref[2], preferred_element_type=f32),
        lax.dot_general(od_ref[0, 0], wb_ref[3], contract_first, preferred_element_type=f32),
    )
    merged = None
    for n in range(N_BRANCH):
        logits = jnp.dot(u, wg_ref[:, n * D_MODEL:(n + 1) * D_MODEL], preferred_element_type=f32)
        term = jax.nn.sigmoid(logits) * branch[n]
        merged = term if merged is None else merged + term
    y = jnp.dot(merged.astype(bf16), wo_ref[...], preferred_element_type=f32)
    z = DEEPNORM_ALPHA * x + (1.0 + gate_ref[0]) * y
    o_ref[0] = _layer_norm(z, g_ref[...], b_ref[...])


def _merge_call(x, shift, scale, gate, o_a, o_b, o_c, lse_c, o_d, w_gate, w_branch, w_out, ln_g, ln_b):
    bsz, seq, d = x.shape
    nt = seq // T
    tok = lambda w: pl.BlockSpec((1, T, w), lambda b, i: (b, i, 0))
    feat = lambda w: pl.BlockSpec((1, 1, w, T), lambda b, i: (b, i, 0, 0))
    mod = pl.BlockSpec((1, 1, d), lambda b, i: (b, 0, 0))

    def kernel(*refs):
        (x_ref, s_ref, c_ref, g_ref, oa, ob, c0, c1, c2, l0, l1, l2, od, wg, wb, wo, lg, lb, o) = refs
        _merge_kernel(x_ref, s_ref, c_ref, g_ref, oa, ob, (c0, c1, c2), (l0, l1, l2), od, wg, wb, wo, lg, lb, o)

    return pl.pallas_call(
        kernel, name="merge_outproj_ln1",
        grid=(bsz, nt),
        in_specs=[tok(d), mod, mod, mod, feat(BRANCH_WIDTH), tok(BRANCH_WIDTH)]
                 + [tok(BRANCH_WIDTH)] * 6 + [feat(BRANCH_WIDTH),
                 _const_spec(w_gate.shape), _const_spec(w_branch.shape), _const_spec(w_out.shape),
                 _const_spec((1, d)), _const_spec((1, d))],
        out_specs=tok(d),
        out_shape=jax.ShapeDtypeStruct((bsz, seq, d), f32),
        compiler_params=_params(2),
    )(x, shift, scale, gate, o_a, o_b, *o_c, *lse_c, o_d, w_gate, w_branch, w_out,
      ln_g.reshape(1, d), ln_b.reshape(1, d))


ROUTER_TILE = 512


def _router_kernel(x_ref, shift_ref, scale_ref, wr_ref, br_ref, o_ref):
    v = _modulate(x_ref, shift_ref, scale_ref)
    logits = lax.dot_general(wr_ref[...], v, (((1,), (1,)), ((), ())), preferred_element_type=f32,
                             precision=lax.Precision.HIGHEST)
    e = jnp.exp(logits - jnp.max(logits, axis=0, keepdims=True))
    probs = e / jnp.sum(e, axis=0, keepdims=True)
    score = probs + br_ref[...]
    p = [probs[n:n + 1] for n in range(N_EXPERTS)]
    s = [score[n:n + 1] for n in range(N_EXPERTS)]
    best_g = None
    for g in range(N_GROUPS):
        a, b, c, d = s[g * EXPERTS_PER_GROUP:(g + 1) * EXPERTS_PER_GROUP]
        hi1, lo1, hi2, lo2 = jnp.maximum(a, b), jnp.minimum(a, b), jnp.maximum(c, d), jnp.minimum(c, d)
        gs = jnp.maximum(hi1, hi2) + jnp.maximum(jnp.minimum(hi1, hi2), jnp.maximum(lo1, lo2))
        if best_g is None:
            best_g, grp = gs, jnp.zeros_like(gs, dtype=i32)
        else:
            better = gs > best_g
            best_g = jnp.where(better, gs, best_g)
            grp = jnp.where(better, g, grp)
    neg_inf = jnp.float32(-jnp.inf)
    cand = [jnp.where(grp == n // EXPERTS_PER_GROUP, s[n], neg_inf) for n in range(N_EXPERTS)]

    def first_argmax(vals):
        best, idx = vals[0], jnp.zeros_like(grp)
        for n in range(1, N_EXPERTS):
            better = vals[n] > best
            best = jnp.where(better, vals[n], best)
            idx = jnp.where(better, n, idx)
        return idx

    i1 = first_argmax(cand)
    i2 = first_argmax([jnp.where(i1 == n, neg_inf, cand[n]) for n in range(N_EXPERTS)])
    w1 = sum(jnp.where(i1 == n, p[n], 0.0) for n in range(N_EXPERTS))
    w2 = sum(jnp.where(i2 == n, p[n], 0.0) for n in range(N_EXPERTS))
    tot = w1 + w2
    rows = [jnp.where(i1 == n, w1 / tot, jnp.where(i2 == n, w2 / tot, 0.0)) for n in range(N_EXPERTS)]
    o_ref[0] = jnp.concatenate(rows, axis=0)


def _router_call(x, shift, scale, w_router, b_router):
    bsz, seq, d = x.shape
    rt = min(ROUTER_TILE, seq)
    mod = pl.BlockSpec((1, 1, d), lambda b, i: (b, 0, 0))
    return pl.pallas_call(
        _router_kernel, name="moe_router",
        grid=(bsz, seq // rt),
        in_specs=[pl.BlockSpec((1, rt, d), lambda b, i: (b, i, 0)), mod, mod,
                  _const_spec((N_EXPERTS, d)), _const_spec((N_EXPERTS, 1))],
        out_specs=pl.BlockSpec((1, N_EXPERTS, rt), lambda b, i: (b, 0, i)),
        out_shape=jax.ShapeDtypeStruct((bsz, N_EXPERTS, seq), f32),
        compiler_params=_params(2),
    )(x, shift, scale, w_router.T.astype(f32), b_router.astype(f32).reshape(N_EXPERTS, 1))


def _moe_kernel(x_ref, shift_ref, scale_ref, gate_ref, wt_ref, wg_ref, wu_ref, wd_ref, g_ref, b_ref, o_ref):
    x = x_ref[0]
    v = _modulate(x_ref, shift_ref, scale_ref).astype(bf16)
    wt = wt_ref[0]
    parts = []
    for n in range(N_EXPERTS):
        cols = slice(n * D_FF_EXPERT, (n + 1) * D_FF_EXPERT)
        hg = jnp.dot(v, wg_ref[:, cols], preferred_element_type=f32)
        hu = jnp.dot(v, wu_ref[:, cols], preferred_element_type=f32)
        w_n = wt[:, n:n + 1]
        act = hg * jax.nn.sigmoid(hg) * hu
        parts.append(jnp.where(w_n != 0.0, act * w_n, 0.0).astype(bf16))
    h = jnp.concatenate(parts, axis=1)
    y = jnp.dot(h, wd_ref[...], preferred_element_type=f32)
    z = DEEPNORM_ALPHA * x + (1.0 + gate_ref[0]) * y
    o_ref[0] = _layer_norm(z, g_ref[...], b_ref[...])


def _moe_call(x, shift, scale, gate, wt, w_gate, w_up, w_down, ln_g, ln_b):
    bsz, seq, d = x.shape
    tok = lambda w: pl.BlockSpec((1, T, w), lambda b, i: (b, i, 0))
    mod = pl.BlockSpec((1, 1, d), lambda b, i: (b, 0, 0))
    return pl.pallas_call(
        _moe_kernel, name="moe_experts_ln2",
        grid=(bsz, seq // T),
        in_specs=[tok(d), mod, mod, mod, tok(N_EXPERTS),
                  _const_spec(w_gate.shape), _const_spec(w_up.shape), _const_spec(w_down.shape),
                  _const_spec((1, d)), _const_spec((1, d))],
        out_specs=tok(d),
        out_shape=jax.ShapeDtypeStruct((bsz, seq, d), f32),
        compiler_params=_params(2),
    )(x, shift, scale, gate, wt, w_gate, w_up, w_down, ln_g.reshape(1, d), ln_b.reshape(1, d))


def _t5_bucket(n):
    max_exact = REL_BUCKETS // 2
    nf = jnp.maximum(n, 1).astype(f32)
    log_ratio = jnp.log(nf / max_exact) / math.log(REL_MAX_DIST / max_exact)
    large = max_exact + (log_ratio * (REL_BUCKETS - max_exact)).astype(i32)
    large = jnp.minimum(large, REL_BUCKETS - 1)
    return jnp.where(n < max_exact, n, large)


def _bias_tables(rel_table, seq):
    n_cols = (N_FAR + 1) * T
    period = n_cols + 1
    n_dist = max(seq, period)
    by_dist = rel_table.astype(f32)[_t5_bucket(jnp.arange(n_dist, dtype=i32))]
    heads = list(range(A_BIAS0, A_BIAS0 + A_HEADS)) + list(range(D_BIAS0, D_BIAS0 + D_HEADS))
    vec = by_dist[:period, np.array(heads)].T * LOG2E
    shifted = jnp.tile(vec, (1, T))[:, :T * (period - 1)].reshape(len(heads), T, period - 1)
    dense = shifted[:, :, :n_cols].reshape(len(heads), T, N_FAR + 1, T).transpose(0, 2, 1, 3)
    bias_a = dense[:A_HEADS]
    bias_d = dense[A_HEADS:]
    band_dist = np.arange(BAND)[:, None] + BAND - np.arange(2 * BAND)[None, :]

    def band(h0, nh, max_dist, dilation):
        valid = (band_dist >= 0) & (band_dist <= max_dist)
        vals = by_dist[np.clip(band_dist, 0, BAND) * dilation][..., h0:h0 + nh]
        return jnp.moveaxis(jnp.where(valid[..., None], vals, NEG_MASK), -1, 0)

    bias_b = band(B_BIAS0, B_QHEADS, B_WINDOW - 1, 1)
    bias_c = [band(C_BIAS0 + g * C_HEADS, C_HEADS, window // dil, dil)
              for g, (window, dil) in enumerate(C_PATTERNS)]
    return bias_a, bias_b, bias_c, bias_d


def _split_cols(w):
    parts, off = {}, 0
    for name, width in IN_SPLITS:
        parts[name] = w[:, off:off + width]
        off += width
    return parts


def _dilate(a, r):
    if r == 1:
        return a
    b, s, f = a.shape
    return a.reshape(b, s // r, r, f).transpose(0, 2, 1, 3).reshape(b * r, s // r, f)


def _undilate(a, r, bsz):
    if r == 1:
        return a
    _, sn, f = a.shape
    return a.reshape(bsz, r, sn, f).transpose(0, 2, 1, 3).reshape(bsz, sn * r, f)


def _layer(l, x, mod, tables, w_router, b_router, w_in, lam_q, lam_k, subln_g, sinks, w_branch, w_out,
           ln1_g, ln1_b, w_exp_gate, w_exp_up, w_exp_down, ln2_g, ln2_b):
    bsz, seq, d = x.shape
    shift_a, scale_a, gate_a, shift_f, scale_f, gate_f = mod
    bias_a, bias_b, bias_c, bias_d = tables
    w = _split_cols(w_in.astype(bf16))
    hw = C_HEADS * HEAD_DIM

    aq, ak, av = _proj_call("proj_a", x, shift_a, scale_a, [w['a_q'], w['a_k'], w['a_v']],
                            [('feat', A_QK ** -0.5 * LOG2E), ('tokh', 2 * A_HEADS, A_QK), ('feat',)],
                            [bf16] * 3)
    o_a = _diff_attn_call(aq, ak, av, bias_a, lam_q, lam_k, subln_g, l)

    bq, bk, bv = _proj_call("proj_b", x, shift_a, scale_a, [w['b_q'], w['b_k'], w['b_v']],
                            [('tok',)] * 3, [bf16] * 3)
    o_b = _band_call("mixer_b_window_attn", bq, bk, bv, bias_b, sinks)

    c_cols = [w[n][:, g * hw:(g + 1) * hw] for g in range(len(C_PATTERNS)) for n in ('c_q', 'c_k', 'c_v')]
    c_proj = _proj_call("proj_c", x, shift_a, scale_a, c_cols, [('tok',)] * 9, [bf16] * 9)
    o_c, lse_c = [], []
    for g, (_, dil) in enumerate(C_PATTERNS):
        qg, kg, vg = (_dilate(a, dil) for a in c_proj[3 * g:3 * g + 3])
        og, lg = _band_call(f"mixer_c_dilated_attn_{g}", qg, kg, vg, bias_c[g], None)
        o_c.append(_undilate(og, dil, bsz))
        lse_c.append(_undilate(lg, dil, bsz))

    dq, dk, dv, iq, ik, iw = _proj_call(
        "proj_d", x, shift_a, scale_a, [w['d_q'], w['d_k'], w['d_v'], w['i_q'], w['i_k'], w['i_w']],
        [('feat', HEAD_DIM ** -0.5 * LOG2E), ('tok',), ('feat',), ('feat',), ('tok',), ('feat',)],
        [bf16] * 5 + [f32])
    o_d = _sparse_attn_call(dq, dk, dv, iq, ik, iw, bias_d)

    x1 = _merge_call(x, shift_a, scale_a, gate_a, o_a, o_b, o_c, lse_c, o_d, w['gate'],
                     w_branch.astype(bf16), w_out.astype(bf16), ln1_g, ln1_b)

    wt = _router_call(x1, shift_f, scale_f, w_router, b_router)
    wt = jnp.swapaxes(wt, 1, 2)
    n_ff = N_EXPERTS * D_FF_EXPERT
    wg_all = jnp.moveaxis(w_exp_gate, 0, 1).reshape(d, n_ff).astype(bf16)
    wu_all = jnp.moveaxis(w_exp_up, 0, 1).reshape(d, n_ff).astype(bf16)
    wd_all = w_exp_down.reshape(n_ff, d).astype(bf16)
    return _moe_call(x1, shift_f, scale_f, gate_f, wt, wg_all, wu_all, wd_all, ln2_g, ln2_b)


def kernel(x, c, rel_table, w_router, b_router, w_ada, b_ada, w_in, a_lambda_q, a_lambda_k, a_subln_g,
           b_sinks, w_branch, w_out, ln1_g, ln1_b, w_exp_gate, w_exp_up, w_exp_down, ln2_g, ln2_b):
    bsz, seq, d = x.shape
    assert d == D_MODEL and seq % (BAND * C_PATTERNS[-1][1]) == 0 and seq % BAND_TILE == 0
    mod = _ada_call(c, w_ada, b_ada)
    tables = _bias_tables(rel_table, seq)
    for l in range(DEPTH):
        mod_l = tuple(m.reshape(bsz, 1, d) for m in jnp.split(mod[l], 6, axis=-1))
        x = _layer(l, x, mod_l, tables, w_router, b_router, w_in[l], a_lambda_q[l], a_lambda_k[l],
                   a_subln_g[l], b_sinks[l], w_branch[l], w_out[l], ln1_g[l], ln1_b[l],
                   w_exp_gate[l], w_exp_up[l], w_exp_down[l], ln2_g[l], ln2_b[l])
    return x
```

```python
import functools
import math

import numpy as np
import jax
import jax.numpy as jnp
from jax import lax
from jax.experimental import pallas as pl
from jax.experimental.pallas import tpu as pltpu

f32 = jnp.float32
bf16 = jnp.bfloat16
i32 = jnp.int32
i16 = jnp.int16

D_MODEL = 1024
DEPTH = 2
HEAD_DIM = 64
A_HEADS = 4
A_QK = 32
A_V = 2 * A_QK
B_QHEADS = 4
B_KVHEADS = 2
B_WINDOW = 128
C_PATTERNS = ((128, 1), (512, 4), (2048, 16))
C_HEADS = 4
D_HEADS = 4
D_TOPK_MAX = 256
IDX_HEADS = 8
IDX_DIM = 32
REL_BUCKETS = 32
REL_MAX_DIST = 2048
A_BIAS0 = 0
B_BIAS0 = A_BIAS0 + A_HEADS
C_BIAS0 = B_BIAS0 + B_QHEADS
D_BIAS0 = C_BIAS0 + len(C_PATTERNS) * C_HEADS
N_BIAS_HEADS = D_BIAS0 + D_HEADS
N_BRANCH = 4
BRANCH_WIDTH = 256
N_EXPERTS = 16
N_GROUPS = 4
EXPERTS_PER_GROUP = N_EXPERTS // N_GROUPS
D_FF_EXPERT = 256
DEEPNORM_ALPHA = (2 * DEPTH) ** 0.25
LN_EPS = 1e-5

IN_SPLITS = (
    ('a_q', A_HEADS * 2 * A_QK), ('a_k', A_HEADS * 2 * A_QK), ('a_v', A_HEADS * A_V),
    ('b_q', B_QHEADS * HEAD_DIM), ('b_k', B_KVHEADS * HEAD_DIM), ('b_v', B_KVHEADS * HEAD_DIM),
    ('c_q', len(C_PATTERNS) * C_HEADS * HEAD_DIM), ('c_k', len(C_PATTERNS) * C_HEADS * HEAD_DIM),
    ('c_v', len(C_PATTERNS) * C_HEADS * HEAD_DIM),
    ('d_q', D_HEADS * HEAD_DIM), ('d_k', HEAD_DIM), ('d_v', HEAD_DIM),
    ('i_q', IDX_HEADS * IDX_DIM), ('i_k', IDX_DIM), ('i_w', IDX_HEADS),
    ('gate', N_BRANCH * D_MODEL),
)

LANES = 128
SUBLANES = 8
PACK = 16
T = 256
BAND = 128
BAND_TILE = 512
VMEM_LIMIT = 56 * 1024 * 1024
N_FAR = -(-(REL_MAX_DIST + T - 1) // T)

KEY_BLOCKS = 4
LOG2E = math.log2(math.e)

NEG_INIT = -1e30
NEG_MASK = -2e30
INT_MIN = -2 ** 31
I16_MIN = -2 ** 15


def _params(n_axes):
    return pltpu.CompilerParams(dimension_semantics=("arbitrary",) * n_axes,
                                vmem_limit_bytes=VMEM_LIMIT)


def _const_spec(shape):
    nd = len(shape)
    return pl.BlockSpec(shape, lambda *_: (0,) * nd, pipeline_mode=pl.Buffered(1))


def _modulate(x_ref, shift_ref, scale_ref):
    return x_ref[0] * (1.0 + scale_ref[0]) + shift_ref[0]


def _layer_norm(z, g, b):
    mu = jnp.mean(z, axis=-1, keepdims=True)
    zc = z - mu
    var = jnp.mean(zc * zc, axis=-1, keepdims=True)
    return zc * lax.rsqrt(var + LN_EPS) * g + b


def _ada_kernel(c_ref, w_ref, b_ref, o_ref):
    c = c_ref[...]
    cond = c * jax.nn.sigmoid(c)
    o_ref[0] = jnp.dot(cond, w_ref[0], preferred_element_type=f32,
                       precision=lax.Precision.HIGHEST) + b_ref[0]


def _ada_call(c, w_ada, b_ada):
    depth, d, n = w_ada.shape
    bsz = c.shape[0]
    rows = -(-bsz // SUBLANES) * SUBLANES
    c_pad = jnp.zeros((rows, d), f32).at[:bsz].set(c)
    tn = 1536
    out = pl.pallas_call(
        _ada_kernel, name="ada_mod",
        grid=(depth, n // tn),
        in_specs=[pl.BlockSpec((rows, d), lambda l, j: (0, 0)),
                  pl.BlockSpec((1, d, tn), lambda l, j: (l, 0, j)),
                  pl.BlockSpec((1, 1, tn), lambda l, j: (l, 0, j))],
        out_specs=pl.BlockSpec((1, rows, tn), lambda l, j: (l, 0, j)),
        out_shape=jax.ShapeDtypeStruct((depth, rows, n), f32),
        compiler_params=_params(2),
    )(c_pad, w_ada, b_ada.reshape(depth, 1, n))
    return out[:, :bsz]


def _proj_kernel(*refs, kinds):
    n = len(kinds)
    x_ref, shift_ref, scale_ref = refs[:3]
    w_refs = refs[3:3 + n]
    o_refs = refs[3 + n:]
    u = _modulate(x_ref, shift_ref, scale_ref).astype(bf16)
    for kind, w_ref, o_ref in zip(kinds, w_refs, o_refs):
        if kind[0] == 'feat':
            r = lax.dot_general(w_ref[...], u, (((1,), (1,)), ((), ())), preferred_element_type=f32)
            if len(kind) > 1:
                r = r * kind[1]
            o_ref[0, 0] = r.astype(o_ref.dtype)
        elif kind[0] == 'tok':
            r = jnp.dot(u, w_ref[...], preferred_element_type=f32)
            o_ref[0] = r.astype(o_ref.dtype)
        else:
            _, heads, hd = kind
            r = jnp.dot(u, w_ref[...], preferred_element_type=f32)
            for h in range(heads):
                o_ref[0, h] = r[:, h * hd:(h + 1) * hd].astype(o_ref.dtype)


def _proj_call(name, x, shift, scale, w_cols, kinds, dtypes):
    bsz, seq, d = x.shape
    nt = seq // T
    weights, w_specs, out_specs, out_shapes = [], [], [], []
    for w, kind, dt in zip(w_cols, kinds, dtypes):
        feat = w.shape[1]
        if kind[0] == 'feat':
            weights.append(w.T)
            w_specs.append(_const_spec((feat, d)))
            out_specs.append(pl.BlockSpec((1, 1, feat, T), lambda b, i: (b, i, 0, 0)))
            out_shapes.append(jax.ShapeDtypeStruct((bsz, nt, feat, T), dt))
        elif kind[0] == 'tok':
            weights.append(w)
            w_specs.append(_const_spec((d, feat)))
            out_specs.append(pl.BlockSpec((1, T, feat), lambda b, i: (b, i, 0)))
            out_shapes.append(jax.ShapeDtypeStruct((bsz, seq, feat), dt))
        else:
            _, heads, hd = kind
            weights.append(w)
            w_specs.append(_const_spec((d, feat)))
            out_specs.append(pl.BlockSpec((1, heads, T, hd), lambda b, i: (b, 0, i, 0)))
            out_shapes.append(jax.ShapeDtypeStruct((bsz, heads, seq, hd), dt))
    mod_spec = pl.BlockSpec((1, 1, d), lambda b, i: (b, 0, 0))
    return pl.pallas_call(
        functools.partial(_proj_kernel, kinds=tuple(kinds)), name=name,
        grid=(bsz, nt),
        in_specs=[pl.BlockSpec((1, T, d), lambda b, i: (b, i, 0)), mod_spec, mod_spec] + w_specs,
        out_specs=out_specs, out_shape=out_shapes,
        compiler_params=_params(2),
    )(x, shift, scale, *weights)


def _softmax_step(s, v_t, state):
    mx, acc = state
    mx_new = jnp.maximum(mx, jnp.max(s, axis=0, keepdims=True))
    alpha = jnp.exp2(mx - mx_new)
    p = jnp.exp2(s - mx_new).astype(bf16)
    v_ext = jnp.concatenate([v_t, jnp.ones((PACK, v_t.shape[1]), bf16)], axis=0)
    acc = alpha * acc + jnp.dot(v_ext, p, preferred_element_type=f32)
    return mx_new, acc


def _softmax_init(dv):
    return (jnp.full((1, T), NEG_INIT, f32), jnp.zeros((dv + PACK, T), f32))


def _softmax_result(state, dv):
    _, acc = state
    return acc[:dv] / acc[dv:dv + 1]


def _causal_tile():
    key = lax.broadcasted_iota(i32, (T, T), 0)
    qry = lax.broadcasted_iota(i32, (T, T), 1)
    return key <= qry


def _sweep_key_tiles(i, tile_logits, tile_values, state):
    n_full = i // KEY_BLOCKS

    def run(first, n, st, ends_on_diagonal):
        logits = [tile_logits(first + t, ends_on_diagonal and t == n - 1) for t in range(n)]
        for t in range(n):
            v_t = tile_values(first + t)
            st = tuple(_softmax_step(s, v_t, st_h) for s, st_h in zip(logits[t], st))
        return st

    state = lax.fori_loop(0, n_full, lambda js, st: run(js * KEY_BLOCKS, KEY_BLOCKS, st, False), state)
    tail = [functools.partial(lambda n, st: run(n_full * KEY_BLOCKS, n, st, True), n)
            for n in range(1, KEY_BLOCKS + 1)]
    return lax.switch(i - n_full * KEY_BLOCKS, tail, state)


def _diff_attn_kernel(lq_ref, lk_ref, q_ref, k_ref, v_ref, bias_ref, g_ref, o_ref, *, lam_init):
    i = pl.program_id(2)
    q_t = q_ref[0, 0]
    q_parts = (q_t[:A_QK], q_t[A_QK:])
    causal = _causal_tile()

    def tile_logits(j, diagonal):
        start = pl.multiple_of(j * T, T)
        bias = bias_ref[0, jnp.minimum(i - j, N_FAR)]
        out = []
        for m in range(2):
            kb = k_ref[0, m, pl.ds(start, T), :]
            s = jnp.dot(kb, q_parts[m], preferred_element_type=f32) + bias
            out.append(jnp.where(causal, s, NEG_MASK) if diagonal else s)
        return out

    state = _sweep_key_tiles(i, tile_logits, lambda j: v_ref[0, j],
                             (_softmax_init(A_V), _softmax_init(A_V)))
    lq = lq_ref[...]
    lk = lk_ref[...]
    lam = (jnp.exp(jnp.sum(lq[0:1] * lk[0:1], axis=1, keepdims=True))
           - jnp.exp(jnp.sum(lq[1:2] * lk[1:2], axis=1, keepdims=True)) + lam_init)
    o = _softmax_result(state[0], A_V) - lam * _softmax_result(state[1], A_V)
    ms = jnp.mean(o * o, axis=0, keepdims=True)
    o = o * lax.rsqrt(ms + LN_EPS) * g_ref[...] * (1.0 - lam_init)
    o_ref[0, 0] = o.astype(o_ref.dtype)


def _diff_attn_call(q_t, k_h, v_t, bias, lam_q, lam_k, subln_g, layer_idx):
    bsz, nt, _, _ = q_t.shape
    seq = nt * T
    lam_init = 0.8 - 0.6 * math.exp(-0.3 * layer_idx)
    g = jnp.broadcast_to(subln_g.astype(f32)[:, None], (A_V, T))
    nb = bias.shape[1]
    return pl.pallas_call(
        functools.partial(_diff_attn_kernel, lam_init=lam_init), name="mixer_a_diff_attn",
        grid=(bsz, A_HEADS, nt),
        in_specs=[_const_spec((2, A_QK)), _const_spec((2, A_QK)),
                  pl.BlockSpec((1, 1, 2 * A_QK, T), lambda b, h, i: (b, i, h, 0)),
                  pl.BlockSpec((1, 2, seq, A_QK), lambda b, h, i: (b, h, 0, 0)),
                  pl.BlockSpec((1, nt, A_V, T), lambda b, h, i: (b, 0, h, 0)),
                  pl.BlockSpec((1, nb, T, T), lambda b, h, i: (h, 0, 0, 0)),
                  _const_spec((A_V, T))],
        out_specs=pl.BlockSpec((1, 1, A_V, T), lambda b, h, i: (b, i, h, 0)),
        out_shape=jax.ShapeDtypeStruct((bsz, nt, A_HEADS * A_V, T), bf16),
        compiler_params=_params(3),
    )(lam_q.astype(f32), lam_k.astype(f32), q_t, k_h, v_t, bias, g)


def _sparse_attn_kernel(q_ref, k_ref, v_ref, iq_ref, ik_ref, iw_ref, bias_ref, o_ref,
                        key_ref, hi_ref, lo_ref, *, k_sel, seq):
    i = pl.program_id(1)
    causal = _causal_tile()
    key_iota = lax.broadcasted_iota(i32, (T, T), 0)
    iq_t = iq_ref[0, 0]
    iw_t = iw_ref[0, 0]

    def score_block(j, diagonal):
        start = pl.multiple_of(j * T, T)
        ikb = ik_ref[0, pl.ds(start, T), :]
        sc = jnp.zeros((T, T), f32)
        for h in range(IDX_HEADS):
            r = jnp.dot(ikb, iq_t[h * IDX_DIM:(h + 1) * IDX_DIM], preferred_element_type=f32)
            sc = sc + jnp.maximum(r, 0.0) * iw_t[h:h + 1]
        bits = pltpu.bitcast(sc, i32)
        key = bits ^ ((bits >> 31) & jnp.int32(0x7FFFFFFF))
        if diagonal:
            key = jnp.where(causal, key, jnp.int32(INT_MIN))
        key_ref[pl.ds(start, T), :] = key
        hi_ref[pl.ds(start, T), :] = (key >> 16).astype(i16)
        lo_ref[pl.ds(start, T), :] = (((key ^ jnp.int32(0x8000)) << 16) >> 16).astype(i16)

    def _score_body(j, carry):
        score_block(j, False)
        return carry

    lax.fori_loop(0, i, _score_body, 0)
    score_block(i, True)
    after = pl.multiple_of((i + 1) * T, T)
    hi_ref[pl.ds(after, T), :] = jnp.full((T, T), I16_MIN, i16)
    lo_ref[pl.ds(after, T), :] = jnp.full((T, T), I16_MIN, i16)
    n_chunks = (i + 2) // 2

    def count16(ref, cand):
        def body(c, acc):
            start = pl.multiple_of(c * 2 * T, 2 * T)
            ind = jnp.where(ref[pl.ds(start, 2 * T), :] >= cand, jnp.int16(1), jnp.int16(0))
            for r in range(2 * T // PACK):
                acc = acc + ind[r * PACK:(r + 1) * PACK]
            return acc
        acc = lax.fori_loop(0, n_chunks, body, jnp.zeros((PACK, T), i16))
        return acc.astype(i32).sum(axis=0, keepdims=True)

    def search16(ref, n_start):
        def bit_step(b, carry):
            t_u, n_ge = carry
            cand_u = t_u | lax.shift_left(jnp.int32(1), 15 - b)
            cnt = count16(ref, (cand_u + I16_MIN).astype(i16))
            take = cnt >= k_sel
            return jnp.where(take, cand_u, t_u), jnp.where(take, cnt, n_ge)

        return lax.fori_loop(0, 16, bit_step, (jnp.zeros((1, T), i32), n_start))

    def any_above_k(n):
        return jnp.max(n.astype(f32)) > k_sel

    n_valid = i * T + lax.broadcasted_iota(i32, (1, T), 1) + 1
    hi_u, n_hi = search16(hi_ref, n_valid)

    def refine():
        p16 = (hi_u + I16_MIN).astype(i16)

        def body(c, carry):
            rows = pl.ds(pl.multiple_of(c * 2 * T, 2 * T), 2 * T)
            hi = hi_ref[rows, :]
            lo_ref[rows, :] = jnp.where(hi > p16, jnp.int16(-I16_MIN - 1),
                                        jnp.where(hi == p16, lo_ref[rows, :], jnp.int16(I16_MIN)))
            return carry

        lax.fori_loop(0, n_chunks, body, 0)
        return search16(lo_ref, n_hi)

    lo_u, n_ge = lax.cond(any_above_k(n_hi), refine, lambda: (jnp.zeros((1, T), i32), n_hi))
    thr = jnp.maximum(((hi_u + I16_MIN) << 16) + lo_u, jnp.int32(INT_MIN + 1))

    tied = n_ge > k_sel

    def count(pred):
        def body(j, acc):
            start = pl.multiple_of(j * T, T)
            ind = jnp.where(pred(key_ref[pl.ds(start, T), :], start), 1, 0).astype(i32)
            return acc + ind.reshape(T // SUBLANES, SUBLANES, T).sum(axis=0)
        acc = lax.fori_loop(0, i + 1, body, jnp.zeros((SUBLANES, T), i32))
        return acc.sum(axis=0, keepdims=True)

    def tie_limit():
        need = k_sel - count(lambda kb, _: kb > thr)
        n_bits = max(1, (seq - 1).bit_length())

        def idx_step(b, lim):
            bit = lax.shift_left(jnp.int32(1), n_bits - 1 - b)
            probe = lim + bit - 1
            cnt = count(lambda kb, start: jnp.logical_and(kb == thr, key_iota + start <= probe))
            return jnp.where(cnt < need, lim + bit, lim)

        lim = lax.fori_loop(0, n_bits, idx_step, jnp.zeros((1, T), i32))
        return jnp.where(tied, lim, jnp.int32(seq))

    q_t = q_ref[0, 0]

    def attend(idx_lim):
        def tile_logits(j, diagonal):
            del diagonal
            start = pl.multiple_of(j * T, T)
            kb = k_ref[0, pl.ds(start, T), :]
            keys = key_ref[pl.ds(start, T), :]
            drop = keys < thr
            if idx_lim is not None:
                drop = jnp.logical_or(drop, jnp.logical_and(keys == thr, key_iota + start > idx_lim))
            out = []
            for h in range(D_HEADS):
                s = jnp.dot(kb, q_t[h * HEAD_DIM:(h + 1) * HEAD_DIM], preferred_element_type=f32)
                out.append(jnp.where(drop, NEG_MASK, s + bias_ref[h, jnp.minimum(i - j, N_FAR)]))
            return out

        return _sweep_key_tiles(i, tile_logits, lambda j: v_ref[0, j],
                                tuple(_softmax_init(HEAD_DIM) for _ in range(D_HEADS)))

    state = lax.cond(any_above_k(n_ge), lambda: attend(tie_limit()), lambda: attend(None))
    for h in range(D_HEADS):
        o_ref[0, 0, h * HEAD_DIM:(h + 1) * HEAD_DIM, :] = (
            _softmax_result(state[h], HEAD_DIM).astype(o_ref.dtype))


def _sparse_attn_call(q_t, k, v_t, iq_t, ik, iw_t, bias):
    bsz, nt, _, _ = q_t.shape
    seq = nt * T
    assert seq // PACK < -I16_MIN
    k_sel = min(D_TOPK_MAX, seq // 4)
    tile = lambda f: pl.BlockSpec((1, 1, f, T), lambda b, i: (b, i, 0, 0))
    whole = lambda shape: pl.BlockSpec((1,) + shape, lambda b, i: (b,) + (0,) * len(shape))
    return pl.pallas_call(
        functools.partial(_sparse_attn_kernel, k_sel=k_sel, seq=seq), name="mixer_d_sparse_attn",
        grid=(bsz, nt),
        in_specs=[tile(D_HEADS * HEAD_DIM), whole((seq, HEAD_DIM)), whole((nt, HEAD_DIM, T)),
                  tile(IDX_HEADS * IDX_DIM), whole((seq, IDX_DIM)), tile(IDX_HEADS),
                  _const_spec(bias.shape)],
        out_specs=tile(D_HEADS * HEAD_DIM),
        out_shape=jax.ShapeDtypeStruct((bsz, nt, D_HEADS * HEAD_DIM, T), bf16),
        scratch_shapes=[pltpu.VMEM((seq, T), i32), pltpu.VMEM((seq + T, T), i16),
                        pltpu.VMEM((seq + T, T), i16)],
        compiler_params=_params(2),
    )(q_t, k, v_t, iq_t, ik, iw_t, bias)


def _band_kernel(*refs, hq, hk, has_sink, tb):
    if has_sink:
        sink_ref, q_ref, kp_ref, kc_ref, vp_ref, vc_ref, bias_ref, o_ref = refs
    else:
        q_ref, kp_ref, kc_ref, vp_ref, vc_ref, bias_ref, o_ref, lse_ref = refs
    i = pl.program_id(1)
    rep = hq // hk
    from_prev_tile = lax.broadcasted_iota(i32, (BAND, 2 * BAND), 1) < BAND
    contract_last = (((1,), (1,)), ((), ()))
    for a in range(tb // BAND):
        if a == 0:
            kk = jnp.concatenate([kp_ref[0], kc_ref[0, 0:BAND, :]], axis=0)
            vv = jnp.concatenate([vp_ref[0], vc_ref[0, 0:BAND, :]], axis=0)
        else:
            kk = kc_ref[0, (a - 1) * BAND:(a + 1) * BAND, :]
            vv = vc_ref[0, (a - 1) * BAND:(a + 1) * BAND, :]
        qa = q_ref[0, a * BAND:(a + 1) * BAND, :]
        outs, lses = [], []
        for h in range(hq):
            g = h // rep
            hs = slice(h * HEAD_DIM, (h + 1) * HEAD_DIM)
            gs = slice(g * HEAD_DIM, (g + 1) * HEAD_DIM)
            s = lax.dot_general(qa[:, hs], kk[:, gs], contract_last, preferred_element_type=f32)
            s = s * (HEAD_DIM ** -0.5) + bias_ref[h]
            if a == 0:
                s = jnp.where(jnp.logical_and(from_prev_tile, i == 0), NEG_MASK, s)
            m = jnp.max(s, axis=1, keepdims=True)
            if has_sink:
                m = jnp.maximum(m, sink_ref[h])
            e = jnp.exp(s - m)
            den = jnp.sum(e, axis=1, keepdims=True)
            if has_sink:
                den = den + jnp.exp(sink_ref[h] - m)
            o = jnp.dot(e.astype(bf16), vv[:, gs], preferred_element_type=f32) / den
            outs.append(o)
            if not has_sink:
                lses.append(jnp.broadcast_to(m + jnp.log(den), (BAND, HEAD_DIM)))
        rows = slice(a * BAND, (a + 1) * BAND)
        o_ref[0, rows, :] = jnp.concatenate(outs, axis=1).astype(o_ref.dtype)
        if not has_sink:
            lse_ref[0, rows, :] = jnp.concatenate(lses, axis=1)


def _band_call(name, q, k, v, bias, sinks):
    bn, sn, qw = q.shape
    kw = k.shape[2]
    hq, hk = qw // HEAD_DIM, kw // HEAD_DIM
    tb = min(BAND_TILE, sn)
    per = tb // BAND
    has_sink = sinks is not None
    cur = lambda w: pl.BlockSpec((1, tb, w), lambda b, i: (b, i, 0))
    prev = lambda w: pl.BlockSpec((1, BAND, w), lambda b, i: (b, jnp.maximum(i * per - 1, 0), 0))
    in_specs = [cur(qw), prev(kw), cur(kw), prev(kw), cur(kw), _const_spec(bias.shape)]
    args = [q, k, k, v, v, bias]
    if has_sink:
        in_specs = [pl.BlockSpec(memory_space=pltpu.SMEM)] + in_specs
        args = [sinks.astype(f32)] + args
        out_specs = cur(qw)
        out_shape = jax.ShapeDtypeStruct((bn, sn, qw), bf16)
    else:
        out_specs = [cur(qw), cur(qw)]
        out_shape = [jax.ShapeDtypeStruct((bn, sn, qw), f32)] * 2
    return pl.pallas_call(
        functools.partial(_band_kernel, hq=hq, hk=hk, has_sink=has_sink, tb=tb), name=name,
        grid=(bn, sn // tb), in_specs=in_specs, out_specs=out_specs, out_shape=out_shape,
        compiler_params=_params(2),
    )(*args)


def _merge_kernel(x_ref, shift_ref, scale_ref, gate_ref, oa_ref, ob_ref, oc_refs, lse_refs, od_ref,
                  wg_ref, wb_ref, wo_ref, g_ref, b_ref, o_ref):
    x = x_ref[0]
    u = _modulate(x_ref, shift_ref, scale_ref).astype(bf16)
    lse = [r[0] for r in lse_refs]
    top = jnp.maximum(jnp.maximum(lse[0], lse[1]), lse[2])
    ws = [jnp.exp(v - top) for v in lse]
    o_c = (ws[0] * oc_refs[0][0] + ws[1] * oc_refs[1][0] + ws[2] * oc_refs[2][0]) / (ws[0] + ws[1] + ws[2])
    contract_first = (((0,), (0,)), ((), ()))
    branch = (
        lax.dot_general(oa_ref[0, 0], wb_ref[0], contract_first, preferred_element_type=f32),
        jnp.dot(ob_ref[0], wb_ref[1], preferred_element_type=f32),
        jnp.dot(o_c.astype(bf16), wb_ref[2], preferred_element_type=f32),
        lax.dot_general(od_ref[0, 0], wb_ref[3], contract_first, preferred_element_type=f32),
    )
    merged = None
    for n in range(N_BRANCH):
        logits = jnp.dot(u, wg_ref[:, n * D_MODEL:(n + 1) * D_MODEL], preferred_element_type=f32)
        term = jax.nn.sigmoid(logits) * branch[n]
        merged = term if merged is None else merged + term
    y = jnp.dot(merged.astype(bf16), wo_ref[...], preferred_element_type=f32)
    z = DEEPNORM_ALPHA * x + (1.0 + gate_ref[0]) * y
    o_ref[0] = _layer_norm(z, g_ref[...], b_ref[...])


def _merge_call(x, shift, scale, gate, o_a, o_b, o_c, lse_c, o_d, w_gate, w_branch, w_out, ln_g, ln_b):
    bsz, seq, d = x.shape
    nt = seq // T
    tok = lambda w: pl.BlockSpec((1, T, w), lambda b, i: (b, i, 0))
    feat = lambda w: pl.BlockSpec((1, 1, w, T), lambda b, i: (b, i, 0, 0))
    mod = pl.BlockSpec((1, 1, d), lambda b, i: (b, 0, 0))

    def kernel(*refs):
        (x_ref, s_ref, c_ref, g_ref, oa, ob, c0, c1, c2, l0, l1, l2, od, wg, wb, wo, lg, lb, o) = refs
        _merge_kernel(x_ref, s_ref, c_ref, g_ref, oa, ob, (c0, c1, c2), (l0, l1, l2), od, wg, wb, wo, lg, lb, o)

    return pl.pallas_call(
        kernel, name="merge_outproj_ln1",
        grid=(bsz, nt),
        in_specs=[tok(d), mod, mod, mod, feat(BRANCH_WIDTH), tok(BRANCH_WIDTH)]
                 + [tok(BRANCH_WIDTH)] * 6 + [feat(BRANCH_WIDTH),
                 _const_spec(w_gate.shape), _const_spec(w_branch.shape), _const_spec(w_out.shape),
                 _const_spec((1, d)), _const_spec((1, d))],
        out_specs=tok(d),
        out_shape=jax.ShapeDtypeStruct((bsz, seq, d), f32),
        compiler_params=_params(2),
    )(x, shift, scale, gate, o_a, o_b, *o_c, *lse_c, o_d, w_gate, w_branch, w_out,
      ln_g.reshape(1, d), ln_b.reshape(1, d))


ROUTER_TILE = 512


def _router_kernel(x_ref, shift_ref, scale_ref, wr_ref, br_ref, o_ref):
    v = _modulate(x_ref, shift_ref, scale_ref)
    logits = lax.dot_general(wr_ref[...], v, (((1,), (1,)), ((), ())), preferred_element_type=f32,
                             precision=lax.Precision.HIGHEST)
    e = jnp.exp(logits - jnp.max(logits, axis=0, keepdims=True))
    probs = e / jnp.sum(e, axis=0, keepdims=True)
    score = probs + br_ref[...]
    p = [probs[n:n + 1] for n in range(N_EXPERTS)]
    s = [score[n:n + 1] for n in range(N_EXPERTS)]
    best_g = None
    for g in range(N_GROUPS):
        a, b, c, d = s[g * EXPERTS_PER_GROUP:(g + 1) * EXPERTS_PER_GROUP]
        hi1, lo1, hi2, lo2 = jnp.maximum(a, b), jnp.minimum(a, b), jnp.maximum(c, d), jnp.minimum(c, d)
        gs = jnp.maximum(hi1, hi2) + jnp.maximum(jnp.minimum(hi1, hi2), jnp.maximum(lo1, lo2))
        if best_g is None:
            best_g, grp = gs, jnp.zeros_like(gs, dtype=i32)
        else:
            better = gs > best_g
            best_g = jnp.where(better, gs, best_g)
            grp = jnp.where(better, g, grp)
    neg_inf = jnp.float32(-jnp.inf)
    cand = [jnp.where(grp == n // EXPERTS_PER_GROUP, s[n], neg_inf) for n in range(N_EXPERTS)]

    def first_argmax(vals):
        best, idx = vals[0], jnp.zeros_like(grp)
        for n in range(1, N_EXPERTS):
            better = vals[n] > best
            best = jnp.where(better, vals[n], best)
            idx = jnp.where(better, n, idx)
        return idx

    i1 = first_argmax(cand)
    i2 = first_argmax([jnp.where(i1 == n, neg_inf, cand[n]) for n in range(N_EXPERTS)])
    w1 = sum(jnp.where(i1 == n, p[n], 0.0) for n in range(N_EXPERTS))
    w2 = sum(jnp.where(i2 == n, p[n], 0.0) for n in range(N_EXPERTS))
    tot = w1 + w2
    rows = [jnp.where(i1 == n, w1 / tot, jnp.where(i2 == n, w2 / tot, 0.0)) for n in range(N_EXPERTS)]
    o_ref[0] = jnp.concatenate(rows, axis=0)


def _router_call(x, shift, scale, w_router, b_router):
    bsz, seq, d = x.shape
    rt = min(ROUTER_TILE, seq)
    mod = pl.BlockSpec((1, 1, d), lambda b, i: (b, 0, 0))
    return pl.pallas_call(
        _router_kernel, name="moe_router",
        grid=(bsz, seq // rt),
        in_specs=[pl.BlockSpec((1, rt, d), lambda b, i: (b, i, 0)), mod, mod,
                  _const_spec((N_EXPERTS, d)), _const_spec((N_EXPERTS, 1))],
        out_specs=pl.BlockSpec((1, N_EXPERTS, rt), lambda b, i: (b, 0, i)),
        out_shape=jax.ShapeDtypeStruct((bsz, N_EXPERTS, seq), f32),
        compiler_params=_params(2),
    )(x, shift, scale, w_router.T.astype(f32), b_router.astype(f32).reshape(N_EXPERTS, 1))


def _moe_kernel(x_ref, shift_ref, scale_ref, gate_ref, wt_ref, wg_ref, wu_ref, wd_ref, g_ref, b_ref, o_ref):
    x = x_ref[0]
    v = _modulate(x_ref, shift_ref, scale_ref).astype(bf16)
    wt = wt_ref[0]
    parts = []
    for n in range(N_EXPERTS):
        hg = jnp.dot(v, wg_ref[n], preferred_element_type=f32)
        hu = jnp.dot(v, wu_ref[n], preferred_element_type=f32)
        w_n = wt[:, n:n + 1]
        act = hg * jax.nn.sigmoid(hg) * hu
        parts.append(jnp.where(w_n != 0.0, act * w_n, 0.0).astype(bf16))
    h = jnp.concatenate(parts, axis=1)
    y = jnp.dot(h, wd_ref[...], preferred_element_type=f32)
    z = DEEPNORM_ALPHA * x + (1.0 + gate_ref[0]) * y
    o_ref[0] = _layer_norm(z, g_ref[...], b_ref[...])


def _moe_call(x, shift, scale, gate, wt, w_gate, w_up, w_down, ln_g, ln_b):
    bsz, seq, d = x.shape
    tok = lambda w: pl.BlockSpec((1, T, w), lambda b, i: (b, i, 0))
    mod = pl.BlockSpec((1, 1, d), lambda b, i: (b, 0, 0))
    return pl.pallas_call(
        _moe_kernel, name="moe_experts_ln2",
        grid=(bsz, seq // T),
        in_specs=[tok(d), mod, mod, mod, tok(N_EXPERTS),
                  _const_spec(w_gate.shape), _const_spec(w_up.shape), _const_spec(w_down.shape),
                  _const_spec((1, d)), _const_spec((1, d))],
        out_specs=tok(d),
        out_shape=jax.ShapeDtypeStruct((bsz, seq, d), f32),
        compiler_params=_params(2),
    )(x, shift, scale, gate, wt, w_gate, w_up, w_down, ln_g.reshape(1, d), ln_b.reshape(1, d))


def _t5_bucket(n):
    max_exact = REL_BUCKETS // 2
    nf = jnp.maximum(n, 1).astype(f32)
    log_ratio = jnp.log(nf / max_exact) / math.log(REL_MAX_DIST / max_exact)
    large = max_exact + (log_ratio * (REL_BUCKETS - max_exact)).astype(i32)
    large = jnp.minimum(large, REL_BUCKETS - 1)
    return jnp.where(n < max_exact, n, large)


def _toeplitz_kernel(v_ref, o_ref):
    rows, width = o_ref.shape[1], o_ref.shape[2]
    x = jnp.broadcast_to(v_ref[0], (rows, width))
    o_ref[0] = pltpu.roll(x, 0, 1, stride=1, stride_axis=0)


def _toeplitz_call(vec, rows):
    heads, width = vec.shape
    return pl.pallas_call(
        _toeplitz_kernel, name="bias_toeplitz",
        grid=(heads,),
        in_specs=[pl.BlockSpec((1, 1, width), lambda h: (h, 0, 0))],
        out_specs=pl.BlockSpec((1, rows, width), lambda h: (h, 0, 0)),
        out_shape=jax.ShapeDtypeStruct((heads, rows, width), f32),
        compiler_params=_params(1),
    )(vec.reshape(heads, 1, width))


def _bias_tables(rel_table, seq):
    n_cols = (N_FAR + 1) * T
    n_dist = max(seq, n_cols)
    by_dist = rel_table.astype(f32)[_t5_bucket(jnp.arange(n_dist, dtype=i32))].T
    dense_heads = np.r_[A_BIAS0:A_BIAS0 + A_HEADS, D_BIAS0:D_BIAS0 + D_HEADS]
    dense = _toeplitz_call(by_dist[dense_heads, :n_cols] * LOG2E, T)
    dense = dense.reshape(len(dense_heads), T, N_FAR + 1, T).transpose(0, 2, 1, 3)
    bias_a = dense[:A_HEADS]
    bias_d = dense[A_HEADS:]
    width = 4 * BAND

    def band_vec(h0, nh, max_dist, dilation):
        vals = by_dist[h0:h0 + nh, 0:(BAND + 1) * dilation:dilation]
        vals = jnp.where(np.arange(BAND + 1) <= max_dist, vals, NEG_MASK)[:, ::-1]
        return jnp.pad(vals, ((0, 0), (BAND - 1, width - 2 * BAND)), constant_values=NEG_MASK)

    vecs = [band_vec(B_BIAS0, B_QHEADS, B_WINDOW - 1, 1)]
    vecs += [band_vec(C_BIAS0 + g * C_HEADS, C_HEADS, window // dil, dil)
             for g, (window, dil) in enumerate(C_PATTERNS)]
    band = _toeplitz_call(jnp.concatenate(vecs, axis=0), BAND)[:, :, BAND - 1:3 * BAND - 1]
    bias_b = band[:B_QHEADS]
    bias_c = [band[B_QHEADS + g * C_HEADS:B_QHEADS + (g + 1) * C_HEADS] for g in range(len(C_PATTERNS))]
    return bias_a, bias_b, bias_c, bias_d


def _split_cols(w):
    parts, off = {}, 0
    for name, width in IN_SPLITS:
        parts[name] = w[:, off:off + width]
        off += width
    return parts


def _dilate(a, r):
    if r == 1:
        return a
    b, s, f = a.shape
    return a.reshape(b, s // r, r, f).transpose(0, 2, 1, 3).reshape(b * r, s // r, f)


def _undilate(a, r, bsz):
    if r == 1:
        return a
    _, sn, f = a.shape
    return a.reshape(bsz, r, sn, f).transpose(0, 2, 1, 3).reshape(bsz, sn * r, f)


def _layer(l, x, mod, tables, w_router, b_router, w_in, lam_q, lam_k, subln_g, sinks, w_branch, w_out,
           ln1_g, ln1_b, w_exp_gate, w_exp_up, w_exp_down, ln2_g, ln2_b):
    bsz, seq, d = x.shape
    shift_a, scale_a, gate_a, shift_f, scale_f, gate_f = mod
    bias_a, bias_b, bias_c, bias_d = tables
    w = _split_cols(w_in.astype(bf16))
    hw = C_HEADS * HEAD_DIM

    aq, ak, av = _proj_call("proj_a", x, shift_a, scale_a, [w['a_q'], w['a_k'], w['a_v']],
                            [('feat', A_QK ** -0.5 * LOG2E), ('tokh', 2 * A_HEADS, A_QK), ('feat',)],
                            [bf16] * 3)
    o_a = _diff_attn_call(aq, ak, av, bias_a, lam_q, lam_k, subln_g, l)

    bq, bk, bv = _proj_call("proj_b", x, shift_a, scale_a, [w['b_q'], w['b_k'], w['b_v']],
                            [('tok',)] * 3, [bf16] * 3)
    o_b = _band_call("mixer_b_window_attn", bq, bk, bv, bias_b, sinks)

    c_cols = [w[n][:, g * hw:(g + 1) * hw] for g in range(len(C_PATTERNS)) for n in ('c_q', 'c_k', 'c_v')]
    c_proj = _proj_call("proj_c", x, shift_a, scale_a, c_cols, [('tok',)] * 9, [bf16] * 9)
    o_c, lse_c = [], []
    for g, (_, dil) in enumerate(C_PATTERNS):
        qg, kg, vg = (_dilate(a, dil) for a in c_proj[3 * g:3 * g + 3])
        og, lg = _band_call(f"mixer_c_dilated_attn_{g}", qg, kg, vg, bias_c[g], None)
        o_c.append(_undilate(og, dil, bsz))
        lse_c.append(_undilate(lg, dil, bsz))

    dq, dk, dv, iq, ik, iw = _proj_call(
        "proj_d", x, shift_a, scale_a, [w['d_q'], w['d_k'], w['d_v'], w['i_q'], w['i_k'], w['i_w']],
        [('feat', HEAD_DIM ** -0.5 * LOG2E), ('tok',), ('feat',), ('feat',), ('tok',), ('feat',)],
        [bf16] * 5 + [f32])
    o_d = _sparse_attn_call(dq, dk, dv, iq, ik, iw, bias_d)

    x1 = _merge_call(x, shift_a, scale_a, gate_a, o_a, o_b, o_c, lse_c, o_d, w['gate'],
                     w_branch.astype(bf16), w_out.astype(bf16), ln1_g, ln1_b)

    wt = _router_call(x1, shift_f, scale_f, w_router, b_router)
    wt = jnp.swapaxes(wt, 1, 2)
    wd_all = w_exp_down.reshape(N_EXPERTS * D_FF_EXPERT, d).astype(bf16)
    return _moe_call(x1, shift_f, scale_f, gate_f, wt, w_exp_gate.astype(bf16), w_exp_up.astype(bf16),
                     wd_all, ln2_g, ln2_b)


def kernel(x, c, rel_table, w_router, b_router, w_ada, b_ada, w_in, a_lambda_q, a_lambda_k, a_subln_g,
           b_sinks, w_branch, w_out, ln1_g, ln1_b, w_exp_gate, w_exp_up, w_exp_down, ln2_g, ln2_b):
    bsz, seq, d = x.shape
    assert d == D_MODEL and seq % (BAND * C_PATTERNS[-1][1]) == 0 and seq % BAND_TILE == 0
    mod = _ada_call(c, w_ada, b_ada)
    tables = _bias_tables(rel_table, seq)
    for l in range(DEPTH):
        mod_l = tuple(m.reshape(bsz, 1, d) for m in jnp.split(mod[l], 6, axis=-1))
        x = _layer(l, x, mod_l, tables, w_router, b_router, w_in[l], a_lambda_q[l], a_lambda_k[l],
                   a_subln_g[l], b_sinks[l], w_branch[l], w_out[l], ln1_g[l], ln1_b[l],
                   w_exp_gate[l], w_exp_up[l], w_exp_down[l], ln2_g[l], ln2_b[l])
    return x
```

```python
import functools
import math

import numpy as np
import jax
import jax.numpy as jnp
from jax import lax
from jax.experimental import pallas as pl
from jax.experimental.pallas import tpu as pltpu

f32 = jnp.float32
bf16 = jnp.bfloat16
i32 = jnp.int32
i16 = jnp.int16

D_MODEL = 1024
DEPTH = 2
HEAD_DIM = 64
A_HEADS = 4
A_QK = 32
A_V = 2 * A_QK
B_QHEADS = 4
B_KVHEADS = 2
B_WINDOW = 128
C_PATTERNS = ((128, 1), (512, 4), (2048, 16))
C_HEADS = 4
D_HEADS = 4
D_TOPK_MAX = 256
IDX_HEADS = 8
IDX_DIM = 32
REL_BUCKETS = 32
REL_MAX_DIST = 2048
A_BIAS0 = 0
B_BIAS0 = A_BIAS0 + A_HEADS
C_BIAS0 = B_BIAS0 + B_QHEADS
D_BIAS0 = C_BIAS0 + len(C_PATTERNS) * C_HEADS
N_BIAS_HEADS = D_BIAS0 + D_HEADS
N_BRANCH = 4
BRANCH_WIDTH = 256
N_EXPERTS = 16
N_GROUPS = 4
EXPERTS_PER_GROUP = N_EXPERTS // N_GROUPS
D_FF_EXPERT = 256
DEEPNORM_ALPHA = (2 * DEPTH) ** 0.25
LN_EPS = 1e-5

IN_SPLITS = (
    ('a_q', A_HEADS * 2 * A_QK), ('a_k', A_HEADS * 2 * A_QK), ('a_v', A_HEADS * A_V),
    ('b_q', B_QHEADS * HEAD_DIM), ('b_k', B_KVHEADS * HEAD_DIM), ('b_v', B_KVHEADS * HEAD_DIM),
    ('c_q', len(C_PATTERNS) * C_HEADS * HEAD_DIM), ('c_k', len(C_PATTERNS) * C_HEADS * HEAD_DIM),
    ('c_v', len(C_PATTERNS) * C_HEADS * HEAD_DIM),
    ('d_q', D_HEADS * HEAD_DIM), ('d_k', HEAD_DIM), ('d_v', HEAD_DIM),
    ('i_q', IDX_HEADS * IDX_DIM), ('i_k', IDX_DIM), ('i_w', IDX_HEADS),
    ('gate', N_BRANCH * D_MODEL),
)

LANES = 128
SUBLANES = 8
PACK = 16
T = 256
BAND = 128
BAND_TILE = 512
VMEM_LIMIT = 56 * 1024 * 1024
N_FAR = -(-(REL_MAX_DIST + T - 1) // T)

KEY_BLOCKS = 4
LOG2E = math.log2(math.e)

NEG_INIT = -1e30
NEG_MASK = -2e30
INT_MIN = -2 ** 31
I16_MIN = -2 ** 15


def _params(n_axes):
    return pltpu.CompilerParams(dimension_semantics=("arbitrary",) * n_axes,
                                vmem_limit_bytes=VMEM_LIMIT)


def _const_spec(shape):
    nd = len(shape)
    return pl.BlockSpec(shape, lambda *_: (0,) * nd, pipeline_mode=pl.Buffered(1))


def _modulate(x_ref, shift_ref, scale_ref):
    return x_ref[0] * (1.0 + scale_ref[0]) + shift_ref[0]


def _layer_norm(z, g, b):
    mu = jnp.mean(z, axis=-1, keepdims=True)
    zc = z - mu
    var = jnp.mean(zc * zc, axis=-1, keepdims=True)
    return zc * lax.rsqrt(var + LN_EPS) * g + b


def _ada_kernel(c_ref, w_ref, b_ref, o_ref):
    c = c_ref[...]
    cond = c * jax.nn.sigmoid(c)
    o_ref[0] = jnp.dot(cond, w_ref[0], preferred_element_type=f32,
                       precision=lax.Precision.HIGHEST) + b_ref[0]


def _ada_call(c, w_ada, b_ada):
    depth, d, n = w_ada.shape
    bsz = c.shape[0]
    rows = -(-bsz // SUBLANES) * SUBLANES
    c_pad = jnp.zeros((rows, d), f32).at[:bsz].set(c)
    tn = 1536
    out = pl.pallas_call(
        _ada_kernel, name="ada_mod",
        grid=(depth, n // tn),
        in_specs=[pl.BlockSpec((rows, d), lambda l, j: (0, 0)),
                  pl.BlockSpec((1, d, tn), lambda l, j: (l, 0, j)),
                  pl.BlockSpec((1, 1, tn), lambda l, j: (l, 0, j))],
        out_specs=pl.BlockSpec((1, rows, tn), lambda l, j: (l, 0, j)),
        out_shape=jax.ShapeDtypeStruct((depth, rows, n), f32),
        compiler_params=_params(2),
    )(c_pad, w_ada, b_ada.reshape(depth, 1, n))
    return out[:, :bsz]


def _proj_kernel(*refs, kinds):
    n = len(kinds)
    x_ref, shift_ref, scale_ref = refs[:3]
    w_refs = refs[3:3 + n]
    o_refs = refs[3 + n:]
    u = _modulate(x_ref, shift_ref, scale_ref).astype(bf16)
    for kind, w_ref, o_ref in zip(kinds, w_refs, o_refs):
        if kind[0] == 'feat':
            r = lax.dot_general(w_ref[...], u, (((1,), (1,)), ((), ())), preferred_element_type=f32)
            if len(kind) > 1:
                r = r * kind[1]
            o_ref[0, 0] = r.astype(o_ref.dtype)
        elif kind[0] == 'tok':
            r = jnp.dot(u, w_ref[...], preferred_element_type=f32)
            o_ref[0] = r.astype(o_ref.dtype)
        else:
            _, heads, hd = kind
            r = jnp.dot(u, w_ref[...], preferred_element_type=f32)
            for h in range(heads):
                o_ref[0, h] = r[:, h * hd:(h + 1) * hd].astype(o_ref.dtype)


def _proj_call(name, x, shift, scale, w_cols, kinds, dtypes):
    bsz, seq, d = x.shape
    nt = seq // T
    weights, w_specs, out_specs, out_shapes = [], [], [], []
    for w, kind, dt in zip(w_cols, kinds, dtypes):
        feat = w.shape[1]
        if kind[0] == 'feat':
            weights.append(w.T)
            w_specs.append(_const_spec((feat, d)))
            out_specs.append(pl.BlockSpec((1, 1, feat, T), lambda b, i: (b, i, 0, 0)))
            out_shapes.append(jax.ShapeDtypeStruct((bsz, nt, feat, T), dt))
        elif kind[0] == 'tok':
            weights.append(w)
            w_specs.append(_const_spec((d, feat)))
            out_specs.append(pl.BlockSpec((1, T, feat), lambda b, i: (b, i, 0)))
            out_shapes.append(jax.ShapeDtypeStruct((bsz, seq, feat), dt))
        else:
            _, heads, hd = kind
            weights.append(w)
            w_specs.append(_const_spec((d, feat)))
            out_specs.append(pl.BlockSpec((1, heads, T, hd), lambda b, i: (b, 0, i, 0)))
            out_shapes.append(jax.ShapeDtypeStruct((bsz, heads, seq, hd), dt))
    mod_spec = pl.BlockSpec((1, 1, d), lambda b, i: (b, 0, 0))
    return pl.pallas_call(
        functools.partial(_proj_kernel, kinds=tuple(kinds)), name=name,
        grid=(bsz, nt),
        in_specs=[pl.BlockSpec((1, T, d), lambda b, i: (b, i, 0)), mod_spec, mod_spec] + w_specs,
        out_specs=out_specs, out_shape=out_shapes,
        compiler_params=_params(2),
    )(x, shift, scale, *weights)


def _softmax_step(s, v_t, state):
    mx, acc = state
    mx_new = jnp.maximum(mx, jnp.max(s, axis=0, keepdims=True))
    alpha = jnp.exp2(mx - mx_new)
    p = jnp.exp2(s - mx_new).astype(bf16)
    v_ext = jnp.concatenate([v_t, jnp.ones((PACK, v_t.shape[1]), bf16)], axis=0)
    acc = alpha * acc + jnp.dot(v_ext, p, preferred_element_type=f32)
    return mx_new, acc


def _softmax_init(dv):
    return (jnp.full((1, T), NEG_INIT, f32), jnp.zeros((dv + PACK, T), f32))


def _softmax_result(state, dv):
    _, acc = state
    return acc[:dv] / acc[dv:dv + 1]


def _causal_tile():
    key = lax.broadcasted_iota(i32, (T, T), 0)
    qry = lax.broadcasted_iota(i32, (T, T), 1)
    return key <= qry


def _sweep_key_tiles(i, raw_logits, mask_logits, tile_values, state, buf_a, buf_b):
    n_heads = len(state)
    n_full = i // KEY_BLOCKS

    def tile_of(g, t):
        j = g * KEY_BLOCKS + t
        return j, jnp.minimum(j, i)

    def fill(buf, g):
        for t in range(KEY_BLOCKS):
            for h, s in enumerate(raw_logits(tile_of(g, t)[1])):
                buf[t * n_heads + h] = s

    def drain(buf, g, st, tail):
        for t in range(KEY_BLOCKS):
            j, jc = tile_of(g, t)
            v_t = tile_values(jc)
            logits = mask_logits(j, jc, [buf[t * n_heads + h] for h in range(n_heads)], tail)
            st = tuple(_softmax_step(s, v_t, st_h) for s, st_h in zip(logits, st))
        return st

    def pair(gp, st):
        g = 2 * gp
        fill(buf_b, g + 1)
        st = drain(buf_a, g, st, False)
        fill(buf_a, g + 2)
        return drain(buf_b, g + 1, st, False)

    fill(buf_a, 0)
    state = lax.fori_loop(0, n_full // 2, pair, state)
    g_last = n_full - n_full % 2

    def one_more(st):
        fill(buf_b, g_last + 1)
        st = drain(buf_a, g_last, st, False)
        return drain(buf_b, g_last + 1, st, True)

    return lax.cond(n_full % 2 == 1, one_more, lambda st: drain(buf_a, g_last, st, True), state)


def _logit_buffers(n_heads):
    return [pltpu.VMEM((KEY_BLOCKS * n_heads, T, T), f32)] * 2


def _diff_attn_kernel(lq_ref, lk_ref, q_ref, k_ref, v_ref, bias_ref, g_ref, o_ref, buf_a, buf_b,
                      *, lam_init):
    i = pl.program_id(2)
    q_t = q_ref[0, 0]
    q_parts = (q_t[:A_QK], q_t[A_QK:])
    key_minus_query = lax.broadcasted_iota(i32, (T, T), 0) - lax.broadcasted_iota(i32, (T, T), 1)

    def raw_logits(j):
        start = pl.multiple_of(j * T, T)
        bias = bias_ref[0, jnp.minimum(i - j, N_FAR)]
        return [jnp.dot(k_ref[0, m, pl.ds(start, T), :], q_parts[m], preferred_element_type=f32) + bias
                for m in range(2)]

    def mask_logits(j, jc, logits, tail):
        if not tail:
            return logits
        keep = key_minus_query <= (i - j) * T
        return [jnp.where(keep, s, NEG_MASK) for s in logits]

    state = _sweep_key_tiles(i, raw_logits, mask_logits, lambda j: v_ref[0, j],
                             (_softmax_init(A_V), _softmax_init(A_V)), buf_a, buf_b)
    lq = lq_ref[...]
    lk = lk_ref[...]
    lam = (jnp.exp(jnp.sum(lq[0:1] * lk[0:1], axis=1, keepdims=True))
           - jnp.exp(jnp.sum(lq[1:2] * lk[1:2], axis=1, keepdims=True)) + lam_init)
    o = _softmax_result(state[0], A_V) - lam * _softmax_result(state[1], A_V)
    ms = jnp.mean(o * o, axis=0, keepdims=True)
    o = o * lax.rsqrt(ms + LN_EPS) * g_ref[...] * (1.0 - lam_init)
    o_ref[0, 0] = o.astype(o_ref.dtype)


def _diff_attn_call(q_t, k_h, v_t, bias, lam_q, lam_k, subln_g, layer_idx):
    bsz, nt, _, _ = q_t.shape
    seq = nt * T
    lam_init = 0.8 - 0.6 * math.exp(-0.3 * layer_idx)
    g = jnp.broadcast_to(subln_g.astype(f32)[:, None], (A_V, T))
    nb = bias.shape[1]
    return pl.pallas_call(
        functools.partial(_diff_attn_kernel, lam_init=lam_init), name="mixer_a_diff_attn",
        grid=(bsz, A_HEADS, nt),
        in_specs=[_const_spec((2, A_QK)), _const_spec((2, A_QK)),
                  pl.BlockSpec((1, 1, 2 * A_QK, T), lambda b, h, i: (b, i, h, 0)),
                  pl.BlockSpec((1, 2, seq, A_QK), lambda b, h, i: (b, h, 0, 0)),
                  pl.BlockSpec((1, nt, A_V, T), lambda b, h, i: (b, 0, h, 0)),
                  pl.BlockSpec((1, nb, T, T), lambda b, h, i: (h, 0, 0, 0)),
                  _const_spec((A_V, T))],
        out_specs=pl.BlockSpec((1, 1, A_V, T), lambda b, h, i: (b, i, h, 0)),
        out_shape=jax.ShapeDtypeStruct((bsz, nt, A_HEADS * A_V, T), bf16),
        scratch_shapes=_logit_buffers(2),
        compiler_params=_params(3),
    )(lam_q.astype(f32), lam_k.astype(f32), q_t, k_h, v_t, bias, g)


def _sparse_attn_kernel(q_ref, k_ref, v_ref, iq_ref, ik_ref, iw_ref, bias_ref, o_ref,
                        key_ref, hi_ref, lo_ref, buf_a, buf_b, *, k_sel, seq):
    i = pl.program_id(1)
    causal = _causal_tile()
    key_iota = lax.broadcasted_iota(i32, (T, T), 0)
    iq_t = iq_ref[0, 0]
    iw_t = iw_ref[0, 0]

    def score_block(j, diagonal):
        start = pl.multiple_of(j * T, T)
        ikb = ik_ref[0, pl.ds(start, T), :]
        sc = jnp.zeros((T, T), f32)
        for h in range(IDX_HEADS):
            r = jnp.dot(ikb, iq_t[h * IDX_DIM:(h + 1) * IDX_DIM], preferred_element_type=f32)
            sc = sc + jnp.maximum(r, 0.0) * iw_t[h:h + 1]
        bits = pltpu.bitcast(sc, i32)
        key = bits ^ ((bits >> 31) & jnp.int32(0x7FFFFFFF))
        if diagonal:
            key = jnp.where(causal, key, jnp.int32(INT_MIN))
        key_ref[pl.ds(start, T), :] = key
        hi_ref[pl.ds(start, T), :] = (key >> 16).astype(i16)
        lo_ref[pl.ds(start, T), :] = (((key ^ jnp.int32(0x8000)) << 16) >> 16).astype(i16)

    def _score_body(j, carry):
        score_block(j, False)
        return carry

    lax.fori_loop(0, i, _score_body, 0)
    score_block(i, True)
    after = pl.multiple_of((i + 1) * T, T)
    hi_ref[pl.ds(after, T), :] = jnp.full((T, T), I16_MIN, i16)
    lo_ref[pl.ds(after, T), :] = jnp.full((T, T), I16_MIN, i16)
    n_chunks = (i + 2) // 2

    def count16(ref, cand):
        def body(c, acc):
            start = pl.multiple_of(c * 2 * T, 2 * T)
            ind = jnp.where(ref[pl.ds(start, 2 * T), :] >= cand, jnp.int16(1), jnp.int16(0))
            for r in range(2 * T // PACK):
                acc = acc + ind[r * PACK:(r + 1) * PACK]
            return acc
        acc = lax.fori_loop(0, n_chunks, body, jnp.zeros((PACK, T), i16))
        return acc.astype(i32).sum(axis=0, keepdims=True)

    def search16(ref, n_start):
        def bit_step(b, carry):
            t_u, n_ge = carry
            cand_u = t_u | lax.shift_left(jnp.int32(1), 15 - b)
            cnt = count16(ref, (cand_u + I16_MIN).astype(i16))
            take = cnt >= k_sel
            return jnp.where(take, cand_u, t_u), jnp.where(take, cnt, n_ge)

        return lax.fori_loop(0, 16, bit_step, (jnp.zeros((1, T), i32), n_start))

    def any_above_k(n):
        return jnp.max(n.astype(f32)) > k_sel

    n_valid = i * T + lax.broadcasted_iota(i32, (1, T), 1) + 1
    hi_u, n_hi = search16(hi_ref, n_valid)

    def refine():
        p16 = (hi_u + I16_MIN).astype(i16)

        def body(c, carry):
            rows = pl.ds(pl.multiple_of(c * 2 * T, 2 * T), 2 * T)
            hi = hi_ref[rows, :]
            lo_ref[rows, :] = jnp.where(hi > p16, jnp.int16(-I16_MIN - 1),
                                        jnp.where(hi == p16, lo_ref[rows, :], jnp.int16(I16_MIN)))
            return carry

        lax.fori_loop(0, n_chunks, body, 0)
        return search16(lo_ref, n_hi)

    lo_u, n_ge = lax.cond(any_above_k(n_hi), refine, lambda: (jnp.zeros((1, T), i32), n_hi))
    thr = jnp.maximum(((hi_u + I16_MIN) << 16) + lo_u, jnp.int32(INT_MIN + 1))

    tied = n_ge > k_sel

    def count(pred):
        def body(j, acc):
            start = pl.multiple_of(j * T, T)
            ind = jnp.where(pred(key_ref[pl.ds(start, T), :], start), 1, 0).astype(i32)
            return acc + ind.reshape(T // SUBLANES, SUBLANES, T).sum(axis=0)
        acc = lax.fori_loop(0, i + 1, body, jnp.zeros((SUBLANES, T), i32))
        return acc.sum(axis=0, keepdims=True)

    def tie_limit():
        need = k_sel - count(lambda kb, _: kb > thr)
        n_bits = max(1, (seq - 1).bit_length())

        def idx_step(b, lim):
            bit = lax.shift_left(jnp.int32(1), n_bits - 1 - b)
            probe = lim + bit - 1
            cnt = count(lambda kb, start: jnp.logical_and(kb == thr, key_iota + start <= probe))
            return jnp.where(cnt < need, lim + bit, lim)

        lim = lax.fori_loop(0, n_bits, idx_step, jnp.zeros((1, T), i32))
        return jnp.where(tied, lim, jnp.int32(seq))

    q_t = q_ref[0, 0]

    def attend(idx_lim):
        def raw_logits(j):
            kb = k_ref[0, pl.ds(pl.multiple_of(j * T, T), T), :]
            return [jnp.dot(kb, q_t[h * HEAD_DIM:(h + 1) * HEAD_DIM], preferred_element_type=f32)
                    + bias_ref[h, jnp.minimum(i - j, N_FAR)] for h in range(D_HEADS)]

        def mask_logits(j, jc, logits, tail):
            start = pl.multiple_of(jc * T, T)
            keys = key_ref[pl.ds(start, T), :]
            drop = keys < thr
            if idx_lim is not None:
                drop = jnp.logical_or(drop, jnp.logical_and(keys == thr, key_iota + start > idx_lim))
            if tail:
                drop = jnp.logical_or(drop, j > i)
            return [jnp.where(drop, NEG_MASK, s) for s in logits]

        return _sweep_key_tiles(i, raw_logits, mask_logits, lambda j: v_ref[0, j],
                                tuple(_softmax_init(HEAD_DIM) for _ in range(D_HEADS)), buf_a, buf_b)

    state = lax.cond(any_above_k(n_ge), lambda: attend(tie_limit()), lambda: attend(None))
    for h in range(D_HEADS):
        o_ref[0, 0, h * HEAD_DIM:(h + 1) * HEAD_DIM, :] = (
            _softmax_result(state[h], HEAD_DIM).astype(o_ref.dtype))


def _sparse_attn_call(q_t, k, v_t, iq_t, ik, iw_t, bias):
    bsz, nt, _, _ = q_t.shape
    seq = nt * T
    assert seq // PACK < -I16_MIN
    k_sel = min(D_TOPK_MAX, seq // 4)
    tile = lambda f: pl.BlockSpec((1, 1, f, T), lambda b, i: (b, i, 0, 0))
    whole = lambda shape: pl.BlockSpec((1,) + shape, lambda b, i: (b,) + (0,) * len(shape))
    return pl.pallas_call(
        functools.partial(_sparse_attn_kernel, k_sel=k_sel, seq=seq), name="mixer_d_sparse_attn",
        grid=(bsz, nt),
        in_specs=[tile(D_HEADS * HEAD_DIM), whole((seq, HEAD_DIM)), whole((nt, HEAD_DIM, T)),
                  tile(IDX_HEADS * IDX_DIM), whole((seq, IDX_DIM)), tile(IDX_HEADS),
                  _const_spec(bias.shape)],
        out_specs=tile(D_HEADS * HEAD_DIM),
        out_shape=jax.ShapeDtypeStruct((bsz, nt, D_HEADS * HEAD_DIM, T), bf16),
        scratch_shapes=[pltpu.VMEM((seq, T), i32), pltpu.VMEM((seq + T, T), i16),
                        pltpu.VMEM((seq + T, T), i16)] + _logit_buffers(D_HEADS),
        compiler_params=_params(2),
    )(q_t, k, v_t, iq_t, ik, iw_t, bias)


def _band_kernel(*refs, hq, hk, has_sink, tb):
    if has_sink:
        sink_ref, q_ref, kp_ref, kc_ref, vp_ref, vc_ref, bias_ref, o_ref = refs
    else:
        q_ref, kp_ref, kc_ref, vp_ref, vc_ref, bias_ref, o_ref, lse_ref = refs
    i = pl.program_id(1)
    rep = hq // hk
    from_prev_tile = lax.broadcasted_iota(i32, (BAND, 2 * BAND), 1) < BAND
    contract_last = (((1,), (1,)), ((), ()))
    for a in range(tb // BAND):
        if a == 0:
            kk = jnp.concatenate([kp_ref[0], kc_ref[0, 0:BAND, :]], axis=0)
            vv = jnp.concatenate([vp_ref[0], vc_ref[0, 0:BAND, :]], axis=0)
        else:
            kk = kc_ref[0, (a - 1) * BAND:(a + 1) * BAND, :]
            vv = vc_ref[0, (a - 1) * BAND:(a + 1) * BAND, :]
        qa = q_ref[0, a * BAND:(a + 1) * BAND, :]
        outs, lses = [], []
        for h in range(hq):
            g = h // rep
            hs = slice(h * HEAD_DIM, (h + 1) * HEAD_DIM)
            gs = slice(g * HEAD_DIM, (g + 1) * HEAD_DIM)
            s = lax.dot_general(qa[:, hs], kk[:, gs], contract_last, preferred_element_type=f32)
            s = s * (HEAD_DIM ** -0.5) + bias_ref[h]
            if a == 0:
                s = jnp.where(jnp.logical_and(from_prev_tile, i == 0), NEG_MASK, s)
            m = jnp.max(s, axis=1, keepdims=True)
            if has_sink:
                m = jnp.maximum(m, sink_ref[h])
            e = jnp.exp(s - m)
            den = jnp.sum(e, axis=1, keepdims=True)
            if has_sink:
                den = den + jnp.exp(sink_ref[h] - m)
            o = jnp.dot(e.astype(bf16), vv[:, gs], preferred_element_type=f32) / den
            outs.append(o)
            if not has_sink:
                lses.append(jnp.broadcast_to(m + jnp.log(den), (BAND, HEAD_DIM)))
        rows = slice(a * BAND, (a + 1) * BAND)
        o_ref[0, rows, :] = jnp.concatenate(outs, axis=1).astype(o_ref.dtype)
        if not has_sink:
            lse_ref[0, rows, :] = jnp.concatenate(lses, axis=1)


def _band_call(name, q, k, v, bias, sinks):
    bn, sn, qw = q.shape
    kw = k.shape[2]
    hq, hk = qw // HEAD_DIM, kw // HEAD_DIM
    tb = min(BAND_TILE, sn)
    per = tb // BAND
    has_sink = sinks is not None
    cur = lambda w: pl.BlockSpec((1, tb, w), lambda b, i: (b, i, 0))
    prev = lambda w: pl.BlockSpec((1, BAND, w), lambda b, i: (b, jnp.maximum(i * per - 1, 0), 0))
    in_specs = [cur(qw), prev(kw), cur(kw), prev(kw), cur(kw), _const_spec(bias.shape)]
    args = [q, k, k, v, v, bias]
    if has_sink:
        in_specs = [pl.BlockSpec(memory_space=pltpu.SMEM)] + in_specs
        args = [sinks.astype(f32)] + args
        out_specs = cur(qw)
        out_shape = jax.ShapeDtypeStruct((bn, sn, qw), bf16)
    else:
        out_specs = [cur(qw), cur(qw)]
        out_shape = [jax.ShapeDtypeStruct((bn, sn, qw), f32)] * 2
    return pl.pallas_call(
        functools.partial(_band_kernel, hq=hq, hk=hk, has_sink=has_sink, tb=tb), name=name,
        grid=(bn, sn // tb), in_specs=in_specs, out_specs=out_specs, out_shape=out_shape,
        compiler_params=_params(2),
    )(*args)


def _merge_kernel(x_ref, shift_ref, scale_ref, gate_ref, oa_ref, ob_ref, oc_refs, lse_refs, od_ref,
                  wg_ref, wb_ref, wo_ref, g_ref, b_ref, o_ref):
    x = x_ref[0]
    u = _modulate(x_ref, shift_ref, scale_ref).astype(bf16)
    lse = [r[0] for r in lse_refs]
    top = jnp.maximum(jnp.maximum(lse[0], lse[1]), lse[2])
    ws = [jnp.exp(v - top) for v in lse]
    o_c = (ws[0] * oc_refs[0][0] + ws[1] * oc_refs[1][0] + ws[2] * oc_refs[2][0]) / (ws[0] + ws[1] + ws[2])
    contract_first = (((0,), (0,)), ((), ()))
    branch = (
        lax.dot_general(oa_ref[0, 0], wb_ref[0], contract_first, preferred_element_type=f32),
        jnp.dot(ob_ref[0], wb_ref[1], preferred_element_type=f32),
        jnp.dot(o_c.astype(bf16), wb_ref[2], preferred_element_type=f32),
        lax.dot_general(od_ref[0, 0], wb_ref[3], contract_first, preferred_element_type=f32),
    )
    merged = None
    for n in range(N_BRANCH):
        logits = jnp.dot(u, wg_ref[:, n * D_MODEL:(n + 1) * D_MODEL], preferred_element_type=f32)
        term = jax.nn.sigmoid(logits) * branch[n]
        merged = term if merged is None else merged + term
    y = jnp.dot(merged.astype(bf16), wo_ref[...], preferred_element_type=f32)
    z = DEEPNORM_ALPHA * x + (1.0 + gate_ref[0]) * y
    o_ref[0] = _layer_norm(z, g_ref[...], b_ref[...])


def _merge_call(x, shift, scale, gate, o_a, o_b, o_c, lse_c, o_d, w_gate, w_branch, w_out, ln_g, ln_b):
    bsz, seq, d = x.shape
    nt = seq // T
    tok = lambda w: pl.BlockSpec((1, T, w), lambda b, i: (b, i, 0))
    feat = lambda w: pl.BlockSpec((1, 1, w, T), lambda b, i: (b, i, 0, 0))
    mod = pl.BlockSpec((1, 1, d), lambda b, i: (b, 0, 0))

    def kernel(*refs):
        (x_ref, s_ref, c_ref, g_ref, oa, ob, c0, c1, c2, l0, l1, l2, od, wg, wb, wo, lg, lb, o) = refs
        _merge_kernel(x_ref, s_ref, c_ref, g_ref, oa, ob, (c0, c1, c2), (l0, l1, l2), od, wg, wb, wo, lg, lb, o)

    return pl.pallas_call(
        kernel, name="merge_outproj_ln1",
        grid=(bsz, nt),
        in_specs=[tok(d), mod, mod, mod, feat(BRANCH_WIDTH), tok(BRANCH_WIDTH)]
                 + [tok(BRANCH_WIDTH)] * 6 + [feat(BRANCH_WIDTH),
                 _const_spec(w_gate.shape), _const_spec(w_branch.shape), _const_spec(w_out.shape),
                 _const_spec((1, d)), _const_spec((1, d))],
        out_specs=tok(d),
        out_shape=jax.ShapeDtypeStruct((bsz, seq, d), f32),
        compiler_params=_params(2),
    )(x, shift, scale, gate, o_a, o_b, *o_c, *lse_c, o_d, w_gate, w_branch, w_out,
      ln_g.reshape(1, d), ln_b.reshape(1, d))


ROUTER_TILE = 512


def _router_kernel(x_ref, shift_ref, scale_ref, wr_ref, br_ref, o_ref):
    v = _modulate(x_ref, shift_ref, scale_ref)
    logits = lax.dot_general(wr_ref[...], v, (((1,), (1,)), ((), ())), preferred_element_type=f32,
                             precision=lax.Precision.HIGHEST)
    e = jnp.exp(logits - jnp.max(logits, axis=0, keepdims=True))
    probs = e / jnp.sum(e, axis=0, keepdims=True)
    score = probs + br_ref[...]
    p = [probs[n:n + 1] for n in range(N_EXPERTS)]
    s = [score[n:n + 1] for n in range(N_EXPERTS)]
    best_g = None
    for g in range(N_GROUPS):
        a, b, c, d = s[g * EXPERTS_PER_GROUP:(g + 1) * EXPERTS_PER_GROUP]
        hi1, lo1, hi2, lo2 = jnp.maximum(a, b), jnp.minimum(a, b), jnp.maximum(c, d), jnp.minimum(c, d)
        gs = jnp.maximum(hi1, hi2) + jnp.maximum(jnp.minimum(hi1, hi2), jnp.maximum(lo1, lo2))
        if best_g is None:
            best_g, grp = gs, jnp.zeros_like(gs, dtype=i32)
        else:
            better = gs > best_g
            best_g = jnp.where(better, gs, best_g)
            grp = jnp.where(better, g, grp)
    neg_inf = jnp.float32(-jnp.inf)
    cand = [jnp.where(grp == n // EXPERTS_PER_GROUP, s[n], neg_inf) for n in range(N_EXPERTS)]

    def first_argmax(vals):
        best, idx = vals[0], jnp.zeros_like(grp)
        for n in range(1, N_EXPERTS):
            better = vals[n] > best
            best = jnp.where(better, vals[n], best)
            idx = jnp.where(better, n, idx)
        return idx

    i1 = first_argmax(cand)
    i2 = first_argmax([jnp.where(i1 == n, neg_inf, cand[n]) for n in range(N_EXPERTS)])
    w1 = sum(jnp.where(i1 == n, p[n], 0.0) for n in range(N_EXPERTS))
    w2 = sum(jnp.where(i2 == n, p[n], 0.0) for n in range(N_EXPERTS))
    tot = w1 + w2
    rows = [jnp.where(i1 == n, w1 / tot, jnp.where(i2 == n, w2 / tot, 0.0)) for n in range(N_EXPERTS)]
    o_ref[0] = jnp.concatenate(rows, axis=0)


def _router_call(x, shift, scale, w_router, b_router):
    bsz, seq, d = x.shape
    rt = min(ROUTER_TILE, seq)
    mod = pl.BlockSpec((1, 1, d), lambda b, i: (b, 0, 0))
    return pl.pallas_call(
        _router_kernel, name="moe_router",
        grid=(bsz, seq // rt),
        in_specs=[pl.BlockSpec((1, rt, d), lambda b, i: (b, i, 0)), mod, mod,
                  _const_spec((N_EXPERTS, d)), _const_spec((N_EXPERTS, 1))],
        out_specs=pl.BlockSpec((1, N_EXPERTS, rt), lambda b, i: (b, 0, i)),
        out_shape=jax.ShapeDtypeStruct((bsz, N_EXPERTS, seq), f32),
        compiler_params=_params(2),
    )(x, shift, scale, w_router.T.astype(f32), b_router.astype(f32).reshape(N_EXPERTS, 1))


def _moe_kernel(x_ref, shift_ref, scale_ref, gate_ref, wt_ref, wg_ref, wu_ref, wd_ref, g_ref, b_ref, o_ref):
    x = x_ref[0]
    v = _modulate(x_ref, shift_ref, scale_ref).astype(bf16)
    wt = wt_ref[0]
    parts = []
    for n in range(N_EXPERTS):
        hg = jnp.dot(v, wg_ref[n], preferred_element_type=f32)
        hu = jnp.dot(v, wu_ref[n], preferred_element_type=f32)
        w_n = wt[:, n:n + 1]
        act = hg * jax.nn.sigmoid(hg) * hu
        parts.append(jnp.where(w_n != 0.0, act * w_n, 0.0).astype(bf16))
    h = jnp.concatenate(parts, axis=1)
    y = jnp.dot(h, wd_ref[...], preferred_element_type=f32)
    z = DEEPNORM_ALPHA * x + (1.0 + gate_ref[0]) * y
    o_ref[0] = _layer_norm(z, g_ref[...], b_ref[...])


def _moe_call(x, shift, scale, gate, wt, w_gate, w_up, w_down, ln_g, ln_b):
    bsz, seq, d = x.shape
    tok = lambda w: pl.BlockSpec((1, T, w), lambda b, i: (b, i, 0))
    mod = pl.BlockSpec((1, 1, d), lambda b, i: (b, 0, 0))
    return pl.pallas_call(
        _moe_kernel, name="moe_experts_ln2",
        grid=(bsz, seq // T),
        in_specs=[tok(d), mod, mod, mod, tok(N_EXPERTS),
                  _const_spec(w_gate.shape), _const_spec(w_up.shape), _const_spec(w_down.shape),
                  _const_spec((1, d)), _const_spec((1, d))],
        out_specs=tok(d),
        out_shape=jax.ShapeDtypeStruct((bsz, seq, d), f32),
        compiler_params=_params(2),
    )(x, shift, scale, gate, wt, w_gate, w_up, w_down, ln_g.reshape(1, d), ln_b.reshape(1, d))


def _t5_bucket(n):
    max_exact = REL_BUCKETS // 2
    nf = jnp.maximum(n, 1).astype(f32)
    log_ratio = jnp.log(nf / max_exact) / math.log(REL_MAX_DIST / max_exact)
    large = max_exact + (log_ratio * (REL_BUCKETS - max_exact)).astype(i32)
    large = jnp.minimum(large, REL_BUCKETS - 1)
    return jnp.where(n < max_exact, n, large)


def _toeplitz_kernel(v_ref, o_ref):
    rows, width = o_ref.shape[1], o_ref.shape[2]
    x = jnp.broadcast_to(v_ref[0], (rows, width))
    o_ref[0] = pltpu.roll(x, 0, 1, stride=1, stride_axis=0)


def _toeplitz_call(vec, rows):
    heads, width = vec.shape
    return pl.pallas_call(
        _toeplitz_kernel, name="bias_toeplitz",
        grid=(heads,),
        in_specs=[pl.BlockSpec((1, 1, width), lambda h: (h, 0, 0))],
        out_specs=pl.BlockSpec((1, rows, width), lambda h: (h, 0, 0)),
        out_shape=jax.ShapeDtypeStruct((heads, rows, width), f32),
        compiler_params=_params(1),
    )(vec.reshape(heads, 1, width))


def _bias_tables(rel_table, seq):
    n_cols = (N_FAR + 1) * T
    n_dist = max(seq, n_cols)
    by_dist = rel_table.astype(f32)[_t5_bucket(jnp.arange(n_dist, dtype=i32))].T
    dense_heads = np.r_[A_BIAS0:A_BIAS0 + A_HEADS, D_BIAS0:D_BIAS0 + D_HEADS]
    dense = _toeplitz_call(by_dist[dense_heads, :n_cols] * LOG2E, T)
    dense = dense.reshape(len(dense_heads), T, N_FAR + 1, T).transpose(0, 2, 1, 3)
    bias_a = dense[:A_HEADS]
    bias_d = dense[A_HEADS:]
    width = 4 * BAND

    def band_vec(h0, nh, max_dist, dilation):
        vals = by_dist[h0:h0 + nh, 0:(BAND + 1) * dilation:dilation]
        vals = jnp.where(np.arange(BAND + 1) <= max_dist, vals, NEG_MASK)[:, ::-1]
        return jnp.pad(vals, ((0, 0), (BAND - 1, width - 2 * BAND)), constant_values=NEG_MASK)

    vecs = [band_vec(B_BIAS0, B_QHEADS, B_WINDOW - 1, 1)]
    vecs += [band_vec(C_BIAS0 + g * C_HEADS, C_HEADS, window // dil, dil)
             for g, (window, dil) in enumerate(C_PATTERNS)]
    band = _toeplitz_call(jnp.concatenate(vecs, axis=0), BAND)[:, :, BAND - 1:3 * BAND - 1]
    bias_b = band[:B_QHEADS]
    bias_c = [band[B_QHEADS + g * C_HEADS:B_QHEADS + (g + 1) * C_HEADS] for g in range(len(C_PATTERNS))]
    return bias_a, bias_b, bias_c, bias_d


def _split_cols(w):
    parts, off = {}, 0
    for name, width in IN_SPLITS:
        parts[name] = w[:, off:off + width]
        off += width
    return parts


def _dilate(a, r):
    if r == 1:
        return a
    b, s, f = a.shape
    return a.reshape(b, s // r, r, f).transpose(0, 2, 1, 3).reshape(b * r, s // r, f)


def _undilate(a, r, bsz):
    if r == 1:
        return a
    _, sn, f = a.shape
    return a.reshape(bsz, r, sn, f).transpose(0, 2, 1, 3).reshape(bsz, sn * r, f)


def _layer(l, x, mod, tables, w_router, b_router, w_in, lam_q, lam_k, subln_g, sinks, w_branch, w_out,
           ln1_g, ln1_b, w_exp_gate, w_exp_up, w_exp_down, ln2_g, ln2_b):
    bsz, seq, d = x.shape
    shift_a, scale_a, gate_a, shift_f, scale_f, gate_f = mod
    bias_a, bias_b, bias_c, bias_d = tables
    w = _split_cols(w_in.astype(bf16))
    hw = C_HEADS * HEAD_DIM

    aq, ak, av = _proj_call("proj_a", x, shift_a, scale_a, [w['a_q'], w['a_k'], w['a_v']],
                            [('feat', A_QK ** -0.5 * LOG2E), ('tokh', 2 * A_HEADS, A_QK), ('feat',)],
                            [bf16] * 3)
    o_a = _diff_attn_call(aq, ak, av, bias_a, lam_q, lam_k, subln_g, l)

    bq, bk, bv = _proj_call("proj_b", x, shift_a, scale_a, [w['b_q'], w['b_k'], w['b_v']],
                            [('tok',)] * 3, [bf16] * 3)
    o_b = _band_call("mixer_b_window_attn", bq, bk, bv, bias_b, sinks)

    c_cols = [w[n][:, g * hw:(g + 1) * hw] for g in range(len(C_PATTERNS)) for n in ('c_q', 'c_k', 'c_v')]
    c_proj = _proj_call("proj_c", x, shift_a, scale_a, c_cols, [('tok',)] * 9, [bf16] * 9)
    o_c, lse_c = [], []
    for g, (_, dil) in enumerate(C_PATTERNS):
        qg, kg, vg = (_dilate(a, dil) for a in c_proj[3 * g:3 * g + 3])
        og, lg = _band_call(f"mixer_c_dilated_attn_{g}", qg, kg, vg, bias_c[g], None)
        o_c.append(_undilate(og, dil, bsz))
        lse_c.append(_undilate(lg, dil, bsz))

    dq, dk, dv, iq, ik, iw = _proj_call(
        "proj_d", x, shift_a, scale_a, [w['d_q'], w['d_k'], w['d_v'], w['i_q'], w['i_k'], w['i_w']],
        [('feat', HEAD_DIM ** -0.5 * LOG2E), ('tok',), ('feat',), ('feat',), ('tok',), ('feat',)],
        [bf16] * 5 + [f32])
    o_d = _sparse_attn_call(dq, dk, dv, iq, ik, iw, bias_d)

    x1 = _merge_call(x, shift_a, scale_a, gate_a, o_a, o_b, o_c, lse_c, o_d, w['gate'],
                     w_branch.astype(bf16), w_out.astype(bf16), ln1_g, ln1_b)

    wt = _router_call(x1, shift_f, scale_f, w_router, b_router)
    wt = jnp.swapaxes(wt, 1, 2)
    wd_all = w_exp_down.reshape(N_EXPERTS * D_FF_EXPERT, d).astype(bf16)
    return _moe_call(x1, shift_f, scale_f, gate_f, wt, w_exp_gate.astype(bf16), w_exp_up.astype(bf16),
                     wd_all, ln2_g, ln2_b)


def kernel(x, c, rel_table, w_router, b_router, w_ada, b_ada, w_in, a_lambda_q, a_lambda_k, a_subln_g,
           b_sinks, w_branch, w_out, ln1_g, ln1_b, w_exp_gate, w_exp_up, w_exp_down, ln2_g, ln2_b):
    bsz, seq, d = x.shape
    assert d == D_MODEL and seq % (BAND * C_PATTERNS[-1][1]) == 0 and seq % BAND_TILE == 0
    mod = _ada_call(c, w_ada, b_ada)
    tables = _bias_tables(rel_table, seq)
    for l in range(DEPTH):
        mod_l = tuple(m.reshape(bsz, 1, d) for m in jnp.split(mod[l], 6, axis=-1))
        x = _layer(l, x, mod_l, tables, w_router, b_router, w_in[l], a_lambda_q[l], a_lambda_k[l],
                   a_subln_g[l], b_sinks[l], w_branch[l], w_out[l], ln1_g[l], ln1_b[l],
                   w_exp_gate[l], w_exp_up[l], w_exp_down[l], ln2_g[l], ln2_b[l])
    return x
```

```python
import functools
import math

import numpy as np
import jax
import jax.numpy as jnp
from jax import lax
from jax.experimental import pallas as pl
from jax.experimental.pallas import tpu as pltpu

f32 = jnp.float32
bf16 = jnp.bfloat16
i32 = jnp.int32
i16 = jnp.int16

D_MODEL = 1024
DEPTH = 2
HEAD_DIM = 64
A_HEADS = 4
A_QK = 32
A_V = 2 * A_QK
B_QHEADS = 4
B_KVHEADS = 2
B_WINDOW = 128
C_PATTERNS = ((128, 1), (512, 4), (2048, 16))
C_HEADS = 4
D_HEADS = 4
D_TOPK_MAX = 256
IDX_HEADS = 8
IDX_DIM = 32
REL_BUCKETS = 32
REL_MAX_DIST = 2048
A_BIAS0 = 0
B_BIAS0 = A_BIAS0 + A_HEADS
C_BIAS0 = B_BIAS0 + B_QHEADS
D_BIAS0 = C_BIAS0 + len(C_PATTERNS) * C_HEADS
N_BIAS_HEADS = D_BIAS0 + D_HEADS
N_BRANCH = 4
BRANCH_WIDTH = 256
N_EXPERTS = 16
N_GROUPS = 4
EXPERTS_PER_GROUP = N_EXPERTS // N_GROUPS
D_FF_EXPERT = 256
DEEPNORM_ALPHA = (2 * DEPTH) ** 0.25
LN_EPS = 1e-5

IN_SPLITS = (
    ('a_q', A_HEADS * 2 * A_QK), ('a_k', A_HEADS * 2 * A_QK), ('a_v', A_HEADS * A_V),
    ('b_q', B_QHEADS * HEAD_DIM), ('b_k', B_KVHEADS * HEAD_DIM), ('b_v', B_KVHEADS * HEAD_DIM),
    ('c_q', len(C_PATTERNS) * C_HEADS * HEAD_DIM), ('c_k', len(C_PATTERNS) * C_HEADS * HEAD_DIM),
    ('c_v', len(C_PATTERNS) * C_HEADS * HEAD_DIM),
    ('d_q', D_HEADS * HEAD_DIM), ('d_k', HEAD_DIM), ('d_v', HEAD_DIM),
    ('i_q', IDX_HEADS * IDX_DIM), ('i_k', IDX_DIM), ('i_w', IDX_HEADS),
    ('gate', N_BRANCH * D_MODEL),
)

LANES = 128
SUBLANES = 8
PACK = 16
T = 256
BAND = 128
BAND_TILE = 512
PROJ_TILE = 512
VMEM_LIMIT = 56 * 1024 * 1024
N_FAR = -(-(REL_MAX_DIST + T - 1) // T)

KEY_BLOCKS = 4
LOG2E = math.log2(math.e)

NEG_INIT = -1e30
NEG_MASK = -2e30
INT_MIN = -2 ** 31
I16_MIN = -2 ** 15


def _params(n_axes):
    return pltpu.CompilerParams(dimension_semantics=("arbitrary",) * n_axes,
                                vmem_limit_bytes=VMEM_LIMIT)


def _const_spec(shape):
    nd = len(shape)
    return pl.BlockSpec(shape, lambda *_: (0,) * nd, pipeline_mode=pl.Buffered(1))


def _modulate(x_ref, shift_ref, scale_ref):
    return x_ref[0] * (1.0 + scale_ref[0]) + shift_ref[0]


def _layer_norm(z, g, b):
    mu = jnp.mean(z, axis=-1, keepdims=True)
    zc = z - mu
    var = jnp.mean(zc * zc, axis=-1, keepdims=True)
    return zc * lax.rsqrt(var + LN_EPS) * g + b


def _ada_kernel(c_ref, w_ref, b_ref, o_ref):
    c = c_ref[...]
    cond = c * jax.nn.sigmoid(c)
    o_ref[0] = jnp.dot(cond, w_ref[0], preferred_element_type=f32,
                       precision=lax.Precision.HIGHEST) + b_ref[0]


def _ada_call(c, w_ada, b_ada):
    depth, d, n = w_ada.shape
    bsz = c.shape[0]
    rows = -(-bsz // SUBLANES) * SUBLANES
    c_pad = jnp.zeros((rows, d), f32).at[:bsz].set(c)
    tn = 1536
    out = pl.pallas_call(
        _ada_kernel, name="ada_mod",
        grid=(depth, n // tn),
        in_specs=[pl.BlockSpec((rows, d), lambda l, j: (0, 0)),
                  pl.BlockSpec((1, d, tn), lambda l, j: (l, 0, j)),
                  pl.BlockSpec((1, 1, tn), lambda l, j: (l, 0, j))],
        out_specs=pl.BlockSpec((1, rows, tn), lambda l, j: (l, 0, j)),
        out_shape=jax.ShapeDtypeStruct((depth, rows, n), f32),
        compiler_params=_params(2),
    )(c_pad, w_ada, b_ada.reshape(depth, 1, n))
    return out[:, :bsz]


def _proj_kernel(*refs, kinds):
    n = len(kinds)
    x_ref, shift_ref, scale_ref = refs[:3]
    w_refs = refs[3:3 + n]
    o_refs = refs[3 + n:]
    u = _modulate(x_ref, shift_ref, scale_ref).astype(bf16)
    for kind, w_ref, o_ref in zip(kinds, w_refs, o_refs):
        if kind[0] == 'feat':
            r = lax.dot_general(w_ref[...], u, (((1,), (1,)), ((), ())), preferred_element_type=f32)
            if len(kind) > 1:
                r = r * kind[1]
            for c in range(o_ref.shape[1]):
                o_ref[0, c] = r[:, c * T:(c + 1) * T].astype(o_ref.dtype)
        elif kind[0] == 'tok':
            r = jnp.dot(u, w_ref[...], preferred_element_type=f32)
            o_ref[0] = r.astype(o_ref.dtype)
        else:
            _, heads, hd = kind
            r = jnp.dot(u, w_ref[...], preferred_element_type=f32)
            for h in range(heads):
                o_ref[0, h] = r[:, h * hd:(h + 1) * hd].astype(o_ref.dtype)


def _proj_call(name, x, shift, scale, w_cols, kinds, dtypes):
    bsz, seq, d = x.shape
    pt = PROJ_TILE
    weights, w_specs, out_specs, out_shapes = [], [], [], []
    for w, kind, dt in zip(w_cols, kinds, dtypes):
        feat = w.shape[1]
        if kind[0] == 'feat':
            weights.append(w.T)
            w_specs.append(_const_spec((feat, d)))
            out_specs.append(pl.BlockSpec((1, pt // T, feat, T), lambda b, i: (b, i, 0, 0)))
            out_shapes.append(jax.ShapeDtypeStruct((bsz, seq // T, feat, T), dt))
        elif kind[0] == 'tok':
            weights.append(w)
            w_specs.append(_const_spec((d, feat)))
            out_specs.append(pl.BlockSpec((1, pt, feat), lambda b, i: (b, i, 0)))
            out_shapes.append(jax.ShapeDtypeStruct((bsz, seq, feat), dt))
        else:
            _, heads, hd = kind
            weights.append(w)
            w_specs.append(_const_spec((d, feat)))
            out_specs.append(pl.BlockSpec((1, heads, pt, hd), lambda b, i: (b, 0, i, 0)))
            out_shapes.append(jax.ShapeDtypeStruct((bsz, heads, seq, hd), dt))
    mod_spec = pl.BlockSpec((1, 1, d), lambda b, i: (b, 0, 0))
    return pl.pallas_call(
        functools.partial(_proj_kernel, kinds=tuple(kinds)), name=name,
        grid=(bsz, seq // pt),
        in_specs=[pl.BlockSpec((1, pt, d), lambda b, i: (b, i, 0)), mod_spec, mod_spec] + w_specs,
        out_specs=out_specs, out_shape=out_shapes,
        compiler_params=_params(2),
    )(x, shift, scale, *weights)


def _softmax_step(s, v_t, state):
    mx, acc = state
    mx_new = jnp.maximum(mx, jnp.max(s, axis=0, keepdims=True))
    alpha = jnp.exp2(mx - mx_new)
    p = jnp.exp2(s - mx_new).astype(bf16)
    v_ext = jnp.concatenate([v_t, jnp.ones((PACK, v_t.shape[1]), bf16)], axis=0)
    acc = alpha * acc + jnp.dot(v_ext, p, preferred_element_type=f32)
    return mx_new, acc


def _softmax_init(dv):
    return (jnp.full((1, T), NEG_INIT, f32), jnp.zeros((dv + PACK, T), f32))


def _softmax_result(state, dv):
    _, acc = state
    return acc[:dv] / acc[dv:dv + 1]


def _causal_tile():
    key = lax.broadcasted_iota(i32, (T, T), 0)
    qry = lax.broadcasted_iota(i32, (T, T), 1)
    return key <= qry


def _sweep_key_tiles(i, raw_logits, mask_logits, tile_values, state, buf_a, buf_b):
    n_heads = len(state)
    n_full = i // KEY_BLOCKS

    def tile_of(g, t):
        j = g * KEY_BLOCKS + t
        return j, jnp.minimum(j, i)

    def fill(buf, g):
        for t in range(KEY_BLOCKS):
            for h, s in enumerate(raw_logits(tile_of(g, t)[1])):
                buf[t * n_heads + h] = s

    def drain(buf, g, st, tail):
        for t in range(KEY_BLOCKS):
            j, jc = tile_of(g, t)
            v_t = tile_values(jc)
            logits = mask_logits(j, jc, [buf[t * n_heads + h] for h in range(n_heads)], tail)
            st = tuple(_softmax_step(s, v_t, st_h) for s, st_h in zip(logits, st))
        return st

    def pair(gp, st):
        g = 2 * gp
        fill(buf_b, g + 1)
        st = drain(buf_a, g, st, False)
        fill(buf_a, g + 2)
        return drain(buf_b, g + 1, st, False)

    fill(buf_a, 0)
    state = lax.fori_loop(0, n_full // 2, pair, state)
    g_last = n_full - n_full % 2

    def one_more(st):
        fill(buf_b, g_last + 1)
        st = drain(buf_a, g_last, st, False)
        return drain(buf_b, g_last + 1, st, True)

    return lax.cond(n_full % 2 == 1, one_more, lambda st: drain(buf_a, g_last, st, True), state)


def _logit_buffers(n_heads):
    return [pltpu.VMEM((KEY_BLOCKS * n_heads, T, T), f32)] * 2


def _diff_attn_kernel(lq_ref, lk_ref, q_ref, k_ref, v_ref, bias_ref, g_ref, o_ref, buf_a, buf_b,
                      *, lam_init):
    i = pl.program_id(2)
    q_t = q_ref[0, 0]
    q_parts = (q_t[:A_QK], q_t[A_QK:])
    key_minus_query = lax.broadcasted_iota(i32, (T, T), 0) - lax.broadcasted_iota(i32, (T, T), 1)

    def raw_logits(j):
        start = pl.multiple_of(j * T, T)
        bias = bias_ref[0, jnp.minimum(i - j, N_FAR)]
        return [jnp.dot(k_ref[0, m, pl.ds(start, T), :], q_parts[m], preferred_element_type=f32) + bias
                for m in range(2)]

    def mask_logits(j, jc, logits, tail):
        if not tail:
            return logits
        keep = key_minus_query <= (i - j) * T
        return [jnp.where(keep, s, NEG_MASK) for s in logits]

    state = _sweep_key_tiles(i, raw_logits, mask_logits, lambda j: v_ref[0, j],
                             (_softmax_init(A_V), _softmax_init(A_V)), buf_a, buf_b)
    lq = lq_ref[...]
    lk = lk_ref[...]
    lam = (jnp.exp(jnp.sum(lq[0:1] * lk[0:1], axis=1, keepdims=True))
           - jnp.exp(jnp.sum(lq[1:2] * lk[1:2], axis=1, keepdims=True)) + lam_init)
    o = _softmax_result(state[0], A_V) - lam * _softmax_result(state[1], A_V)
    ms = jnp.mean(o * o, axis=0, keepdims=True)
    o = o * lax.rsqrt(ms + LN_EPS) * g_ref[...] * (1.0 - lam_init)
    o_ref[0, 0] = o.astype(o_ref.dtype)


def _diff_attn_call(q_t, k_h, v_t, bias, lam_q, lam_k, subln_g, layer_idx):
    bsz, nt, _, _ = q_t.shape
    seq = nt * T
    lam_init = 0.8 - 0.6 * math.exp(-0.3 * layer_idx)
    g = jnp.broadcast_to(subln_g.astype(f32)[:, None], (A_V, T))
    nb = bias.shape[1]
    return pl.pallas_call(
        functools.partial(_diff_attn_kernel, lam_init=lam_init), name="mixer_a_diff_attn",
        grid=(bsz, A_HEADS, nt),
        in_specs=[_const_spec((2, A_QK)), _const_spec((2, A_QK)),
                  pl.BlockSpec((1, 1, 2 * A_QK, T), lambda b, h, i: (b, i, h, 0)),
                  pl.BlockSpec((1, 2, seq, A_QK), lambda b, h, i: (b, h, 0, 0)),
                  pl.BlockSpec((1, nt, A_V, T), lambda b, h, i: (b, 0, h, 0)),
                  pl.BlockSpec((1, nb, T, T), lambda b, h, i: (h, 0, 0, 0)),
                  _const_spec((A_V, T))],
        out_specs=pl.BlockSpec((1, 1, A_V, T), lambda b, h, i: (b, i, h, 0)),
        out_shape=jax.ShapeDtypeStruct((bsz, nt, A_HEADS * A_V, T), bf16),
        scratch_shapes=_logit_buffers(2),
        compiler_params=_params(3),
    )(lam_q.astype(f32), lam_k.astype(f32), q_t, k_h, v_t, bias, g)


def _sparse_attn_kernel(q_ref, k_ref, v_ref, iq_ref, ik_ref, iw_ref, bias_ref, o_ref,
                        key_ref, hi_ref, lo_ref, buf_a, buf_b, *, k_sel, seq):
    i = pl.program_id(1)
    causal = _causal_tile()
    key_iota = lax.broadcasted_iota(i32, (T, T), 0)
    iq_t = iq_ref[0, 0]
    iw_t = iw_ref[0, 0]

    def score_block(j, diagonal):
        start = pl.multiple_of(j * T, T)
        ikb = ik_ref[0, pl.ds(start, T), :]
        sc = jnp.zeros((T, T), f32)
        for h in range(IDX_HEADS):
            r = jnp.dot(ikb, iq_t[h * IDX_DIM:(h + 1) * IDX_DIM], preferred_element_type=f32)
            sc = sc + jnp.maximum(r, 0.0) * iw_t[h:h + 1]
        bits = pltpu.bitcast(sc, i32)
        key = bits ^ ((bits >> 31) & jnp.int32(0x7FFFFFFF))
        if diagonal:
            key = jnp.where(causal, key, jnp.int32(INT_MIN))
        key_ref[pl.ds(start, T), :] = key
        hi_ref[pl.ds(start, T), :] = (key >> 16).astype(i16)
        lo_ref[pl.ds(start, T), :] = (((key ^ jnp.int32(0x8000)) << 16) >> 16).astype(i16)

    def _score_body(j, carry):
        score_block(j, False)
        return carry

    lax.fori_loop(0, i, _score_body, 0)
    score_block(i, True)
    after = pl.multiple_of((i + 1) * T, T)
    hi_ref[pl.ds(after, T), :] = jnp.full((T, T), I16_MIN, i16)
    lo_ref[pl.ds(after, T), :] = jnp.full((T, T), I16_MIN, i16)
    n_chunks = (i + 2) // 2

    def count16(ref, cand):
        def body(c, acc):
            start = pl.multiple_of(c * 2 * T, 2 * T)
            ind = jnp.where(ref[pl.ds(start, 2 * T), :] >= cand, jnp.int16(1), jnp.int16(0))
            for r in range(2 * T // PACK):
                acc = acc + ind[r * PACK:(r + 1) * PACK]
            return acc
        acc = lax.fori_loop(0, n_chunks, body, jnp.zeros((PACK, T), i16))
        return acc.astype(i32).sum(axis=0, keepdims=True)

    def search16(ref, n_start):
        def bit_step(b, carry):
            t_u, n_ge = carry
            cand_u = t_u | lax.shift_left(jnp.int32(1), 15 - b)
            cnt = count16(ref, (cand_u + I16_MIN).astype(i16))
            take = cnt >= k_sel
            return jnp.where(take, cand_u, t_u), jnp.where(take, cnt, n_ge)

        return lax.fori_loop(0, 16, bit_step, (jnp.zeros((1, T), i32), n_start))

    def any_above_k(n):
        return jnp.max(n.astype(f32)) > k_sel

    n_valid = i * T + lax.broadcasted_iota(i32, (1, T), 1) + 1
    hi_u, n_hi = search16(hi_ref, n_valid)

    def refine():
        p16 = (hi_u + I16_MIN).astype(i16)

        def body(c, carry):
            rows = pl.ds(pl.multiple_of(c * 2 * T, 2 * T), 2 * T)
            hi = hi_ref[rows, :]
            lo_ref[rows, :] = jnp.where(hi > p16, jnp.int16(-I16_MIN - 1),
                                        jnp.where(hi == p16, lo_ref[rows, :], jnp.int16(I16_MIN)))
            return carry

        lax.fori_loop(0, n_chunks, body, 0)
        return search16(lo_ref, n_hi)

    lo_u, n_ge = lax.cond(any_above_k(n_hi), refine, lambda: (jnp.zeros((1, T), i32), n_hi))
    thr = jnp.maximum(((hi_u + I16_MIN) << 16) + lo_u, jnp.int32(INT_MIN + 1))

    tied = n_ge > k_sel

    def count(pred):
        def body(j, acc):
            start = pl.multiple_of(j * T, T)
            ind = jnp.where(pred(key_ref[pl.ds(start, T), :], start), 1, 0).astype(i32)
            return acc + ind.reshape(T // SUBLANES, SUBLANES, T).sum(axis=0)
        acc = lax.fori_loop(0, i + 1, body, jnp.zeros((SUBLANES, T), i32))
        return acc.sum(axis=0, keepdims=True)

    def tie_limit():
        need = k_sel - count(lambda kb, _: kb > thr)
        n_bits = max(1, (seq - 1).bit_length())

        def idx_step(b, lim):
            bit = lax.shift_left(jnp.int32(1), n_bits - 1 - b)
            probe = lim + bit - 1
            cnt = count(lambda kb, start: jnp.logical_and(kb == thr, key_iota + start <= probe))
            return jnp.where(cnt < need, lim + bit, lim)

        lim = lax.fori_loop(0, n_bits, idx_step, jnp.zeros((1, T), i32))
        return jnp.where(tied, lim, jnp.int32(seq))

    q_t = q_ref[0, 0]

    def attend(idx_lim):
        def raw_logits(j):
            kb = k_ref[0, pl.ds(pl.multiple_of(j * T, T), T), :]
            return [jnp.dot(kb, q_t[h * HEAD_DIM:(h + 1) * HEAD_DIM], preferred_element_type=f32)
                    + bias_ref[h, jnp.minimum(i - j, N_FAR)] for h in range(D_HEADS)]

        def mask_logits(j, jc, logits, tail):
            start = pl.multiple_of(jc * T, T)
            keys = key_ref[pl.ds(start, T), :]
            drop = keys < thr
            if idx_lim is not None:
                drop = jnp.logical_or(drop, jnp.logical_and(keys == thr, key_iota + start > idx_lim))
            if tail:
                drop = jnp.logical_or(drop, j > i)
            return [jnp.where(drop, NEG_MASK, s) for s in logits]

        return _sweep_key_tiles(i, raw_logits, mask_logits, lambda j: v_ref[0, j],
                                tuple(_softmax_init(HEAD_DIM) for _ in range(D_HEADS)), buf_a, buf_b)

    state = lax.cond(any_above_k(n_ge), lambda: attend(tie_limit()), lambda: attend(None))
    for h in range(D_HEADS):
        o_ref[0, 0, h * HEAD_DIM:(h + 1) * HEAD_DIM, :] = (
            _softmax_result(state[h], HEAD_DIM).astype(o_ref.dtype))


def _sparse_attn_call(q_t, k, v_t, iq_t, ik, iw_t, bias):
    bsz, nt, _, _ = q_t.shape
    seq = nt * T
    assert seq // PACK < -I16_MIN
    k_sel = min(D_TOPK_MAX, seq // 4)
    tile = lambda f: pl.BlockSpec((1, 1, f, T), lambda b, i: (b, i, 0, 0))
    whole = lambda shape: pl.BlockSpec((1,) + shape, lambda b, i: (b,) + (0,) * len(shape))
    return pl.pallas_call(
        functools.partial(_sparse_attn_kernel, k_sel=k_sel, seq=seq), name="mixer_d_sparse_attn",
        grid=(bsz, nt),
        in_specs=[tile(D_HEADS * HEAD_DIM), whole((seq, HEAD_DIM)), whole((nt, HEAD_DIM, T)),
                  tile(IDX_HEADS * IDX_DIM), whole((seq, IDX_DIM)), tile(IDX_HEADS),
                  _const_spec(bias.shape)],
        out_specs=tile(D_HEADS * HEAD_DIM),
        out_shape=jax.ShapeDtypeStruct((bsz, nt, D_HEADS * HEAD_DIM, T), bf16),
        scratch_shapes=[pltpu.VMEM((seq, T), i32), pltpu.VMEM((seq + T, T), i16),
                        pltpu.VMEM((seq + T, T), i16)] + _logit_buffers(D_HEADS),
        compiler_params=_params(2),
    )(q_t, k, v_t, iq_t, ik, iw_t, bias)


def _band_kernel(*refs, hq, hk, has_sink, tb):
    if has_sink:
        sink_ref, q_ref, kp_ref, kc_ref, vp_ref, vc_ref, bias_ref, o_ref = refs
    else:
        q_ref, kp_ref, kc_ref, vp_ref, vc_ref, bias_ref, o_ref, lse_ref = refs
    i = pl.program_id(1)
    rep = hq // hk
    from_prev_tile = lax.broadcasted_iota(i32, (BAND, 2 * BAND), 1) < BAND
    contract_last = (((1,), (1,)), ((), ()))
    n_sub = tb // BAND
    vals, logits = [], []
    for a in range(n_sub):
        if a == 0:
            kk = jnp.concatenate([kp_ref[0], kc_ref[0, 0:BAND, :]], axis=0)
            vv = jnp.concatenate([vp_ref[0], vc_ref[0, 0:BAND, :]], axis=0)
        else:
            kk = kc_ref[0, (a - 1) * BAND:(a + 1) * BAND, :]
            vv = vc_ref[0, (a - 1) * BAND:(a + 1) * BAND, :]
        qa = q_ref[0, a * BAND:(a + 1) * BAND, :]
        row = []
        for h in range(hq):
            g = h // rep
            hs = slice(h * HEAD_DIM, (h + 1) * HEAD_DIM)
            gs = slice(g * HEAD_DIM, (g + 1) * HEAD_DIM)
            s = lax.dot_general(qa[:, hs], kk[:, gs], contract_last, preferred_element_type=f32)
            s = s * (HEAD_DIM ** -0.5) + bias_ref[h]
            if a == 0:
                s = jnp.where(jnp.logical_and(from_prev_tile, i == 0), NEG_MASK, s)
            row.append(s)
        logits.append(row)
        vals.append(vv)
    for a in range(n_sub):
        outs, lses = [], []
        for h in range(hq):
            g = h // rep
            gs = slice(g * HEAD_DIM, (g + 1) * HEAD_DIM)
            s = logits[a][h]
            m = jnp.max(s, axis=1, keepdims=True)
            if has_sink:
                m = jnp.maximum(m, sink_ref[h])
            e = jnp.exp(s - m)
            den = jnp.sum(e, axis=1, keepdims=True)
            if has_sink:
                den = den + jnp.exp(sink_ref[h] - m)
            o = jnp.dot(e.astype(bf16), vals[a][:, gs], preferred_element_type=f32) / den
            outs.append(o)
            if not has_sink:
                lses.append(jnp.broadcast_to(m + jnp.log(den), (BAND, HEAD_DIM)))
        rows = slice(a * BAND, (a + 1) * BAND)
        o_ref[0, rows, :] = jnp.concatenate(outs, axis=1).astype(o_ref.dtype)
        if not has_sink:
            lse_ref[0, rows, :] = jnp.concatenate(lses, axis=1)


def _band_call(name, q, k, v, bias, sinks):
    bn, sn, qw = q.shape
    kw = k.shape[2]
    hq, hk = qw // HEAD_DIM, kw // HEAD_DIM
    tb = min(BAND_TILE, sn)
    per = tb // BAND
    has_sink = sinks is not None
    cur = lambda w: pl.BlockSpec((1, tb, w), lambda b, i: (b, i, 0))
    prev = lambda w: pl.BlockSpec((1, BAND, w), lambda b, i: (b, jnp.maximum(i * per - 1, 0), 0))
    in_specs = [cur(qw), prev(kw), cur(kw), prev(kw), cur(kw), _const_spec(bias.shape)]
    args = [q, k, k, v, v, bias]
    if has_sink:
        in_specs = [pl.BlockSpec(memory_space=pltpu.SMEM)] + in_specs
        args = [sinks.astype(f32)] + args
        out_specs = cur(qw)
        out_shape = jax.ShapeDtypeStruct((bn, sn, qw), bf16)
    else:
        out_specs = [cur(qw), cur(qw)]
        out_shape = [jax.ShapeDtypeStruct((bn, sn, qw), f32)] * 2
    return pl.pallas_call(
        functools.partial(_band_kernel, hq=hq, hk=hk, has_sink=has_sink, tb=tb), name=name,
        grid=(bn, sn // tb), in_specs=in_specs, out_specs=out_specs, out_shape=out_shape,
        compiler_params=_params(2),
    )(*args)


def _merge_kernel(x_ref, shift_ref, scale_ref, gate_ref, oa_ref, ob_ref, oc_refs, lse_refs, od_ref,
                  wg_ref, wb_ref, wo_ref, g_ref, b_ref, o_ref):
    x = x_ref[0]
    u = _modulate(x_ref, shift_ref, scale_ref).astype(bf16)
    lse = [r[0] for r in lse_refs]
    top = jnp.maximum(jnp.maximum(lse[0], lse[1]), lse[2])
    ws = [jnp.exp(v - top) for v in lse]
    o_c = (ws[0] * oc_refs[0][0] + ws[1] * oc_refs[1][0] + ws[2] * oc_refs[2][0]) / (ws[0] + ws[1] + ws[2])
    contract_first = (((0,), (0,)), ((), ()))
    branch = (
        lax.dot_general(oa_ref[0, 0], wb_ref[0], contract_first, preferred_element_type=f32),
        jnp.dot(ob_ref[0], wb_ref[1], preferred_element_type=f32),
        jnp.dot(o_c.astype(bf16), wb_ref[2], preferred_element_type=f32),
        lax.dot_general(od_ref[0, 0], wb_ref[3], contract_first, preferred_element_type=f32),
    )
    merged = None
    for n in range(N_BRANCH):
        logits = jnp.dot(u, wg_ref[:, n * D_MODEL:(n + 1) * D_MODEL], preferred_element_type=f32)
        term = jax.nn.sigmoid(logits) * branch[n]
        merged = term if merged is None else merged + term
    y = jnp.dot(merged.astype(bf16), wo_ref[...], preferred_element_type=f32)
    z = DEEPNORM_ALPHA * x + (1.0 + gate_ref[0]) * y
    o_ref[0] = _layer_norm(z, g_ref[...], b_ref[...])


def _merge_call(x, shift, scale, gate, o_a, o_b, o_c, lse_c, o_d, w_gate, w_branch, w_out, ln_g, ln_b):
    bsz, seq, d = x.shape
    nt = seq // T
    tok = lambda w: pl.BlockSpec((1, T, w), lambda b, i: (b, i, 0))
    feat = lambda w: pl.BlockSpec((1, 1, w, T), lambda b, i: (b, i, 0, 0))
    mod = pl.BlockSpec((1, 1, d), lambda b, i: (b, 0, 0))

    def kernel(*refs):
        (x_ref, s_ref, c_ref, g_ref, oa, ob, c0, c1, c2, l0, l1, l2, od, wg, wb, wo, lg, lb, o) = refs
        _merge_kernel(x_ref, s_ref, c_ref, g_ref, oa, ob, (c0, c1, c2), (l0, l1, l2), od, wg, wb, wo, lg, lb, o)

    return pl.pallas_call(
        kernel, name="merge_outproj_ln1",
        grid=(bsz, nt),
        in_specs=[tok(d), mod, mod, mod, feat(BRANCH_WIDTH), tok(BRANCH_WIDTH)]
                 + [tok(BRANCH_WIDTH)] * 6 + [feat(BRANCH_WIDTH),
                 _const_spec(w_gate.shape), _const_spec(w_branch.shape), _const_spec(w_out.shape),
                 _const_spec((1, d)), _const_spec((1, d))],
        out_specs=tok(d),
        out_shape=jax.ShapeDtypeStruct((bsz, seq, d), f32),
        compiler_params=_params(2),
    )(x, shift, scale, gate, o_a, o_b, *o_c, *lse_c, o_d, w_gate, w_branch, w_out,
      ln_g.reshape(1, d), ln_b.reshape(1, d))


ROUTER_TILE = 512


def _router_kernel(x_ref, shift_ref, scale_ref, wr_ref, br_ref, o_ref):
    v = _modulate(x_ref, shift_ref, scale_ref)
    logits = lax.dot_general(wr_ref[...], v, (((1,), (1,)), ((), ())), preferred_element_type=f32,
                             precision=lax.Precision.HIGHEST)
    e = jnp.exp(logits - jnp.max(logits, axis=0, keepdims=True))
    probs = e / jnp.sum(e, axis=0, keepdims=True)
    score = probs + br_ref[...]
    p = [probs[n:n + 1] for n in range(N_EXPERTS)]
    s = [score[n:n + 1] for n in range(N_EXPERTS)]
    best_g = None
    for g in range(N_GROUPS):
        a, b, c, d = s[g * EXPERTS_PER_GROUP:(g + 1) * EXPERTS_PER_GROUP]
        hi1, lo1, hi2, lo2 = jnp.maximum(a, b), jnp.minimum(a, b), jnp.maximum(c, d), jnp.minimum(c, d)
        gs = jnp.maximum(hi1, hi2) + jnp.maximum(jnp.minimum(hi1, hi2), jnp.maximum(lo1, lo2))
        if best_g is None:
            best_g, grp = gs, jnp.zeros_like(gs, dtype=i32)
        else:
            better = gs > best_g
            best_g = jnp.where(better, gs, best_g)
            grp = jnp.where(better, g, grp)
    neg_inf = jnp.float32(-jnp.inf)
    cand = [jnp.where(grp == n // EXPERTS_PER_GROUP, s[n], neg_inf) for n in range(N_EXPERTS)]

    def first_argmax(vals):
        best, idx = vals[0], jnp.zeros_like(grp)
        for n in range(1, N_EXPERTS):
            better = vals[n] > best
            best = jnp.where(better, vals[n], best)
            idx = jnp.where(better, n, idx)
        return idx

    i1 = first_argmax(cand)
    i2 = first_argmax([jnp.where(i1 == n, neg_inf, cand[n]) for n in range(N_EXPERTS)])
    w1 = sum(jnp.where(i1 == n, p[n], 0.0) for n in range(N_EXPERTS))
    w2 = sum(jnp.where(i2 == n, p[n], 0.0) for n in range(N_EXPERTS))
    tot = w1 + w2
    rows = [jnp.where(i1 == n, w1 / tot, jnp.where(i2 == n, w2 / tot, 0.0)) for n in range(N_EXPERTS)]
    o_ref[0] = jnp.concatenate(rows, axis=0)


def _router_call(x, shift, scale, w_router, b_router):
    bsz, seq, d = x.shape
    rt = min(ROUTER_TILE, seq)
    mod = pl.BlockSpec((1, 1, d), lambda b, i: (b, 0, 0))
    return pl.pallas_call(
        _router_kernel, name="moe_router",
        grid=(bsz, seq // rt),
        in_specs=[pl.BlockSpec((1, rt, d), lambda b, i: (b, i, 0)), mod, mod,
                  _const_spec((N_EXPERTS, d)), _const_spec((N_EXPERTS, 1))],
        out_specs=pl.BlockSpec((1, N_EXPERTS, rt), lambda b, i: (b, 0, i)),
        out_shape=jax.ShapeDtypeStruct((bsz, N_EXPERTS, seq), f32),
        compiler_params=_params(2),
    )(x, shift, scale, w_router.T.astype(f32), b_router.astype(f32).reshape(N_EXPERTS, 1))


GROUP_CAP = 96
MOE_TILE = 512


def _expert_mlp(v, w_cols, experts, wg_ref, wu_ref, wd_ref):
    parts = []
    for w_n, n in zip(w_cols, experts):
        hg = jnp.dot(v, wg_ref[n], preferred_element_type=f32)
        hu = jnp.dot(v, wu_ref[n], preferred_element_type=f32)
        act = hg * jax.nn.sigmoid(hg) * hu
        parts.append(jnp.where(w_n != 0.0, act * w_n, 0.0).astype(bf16))
    h = jnp.concatenate(parts, axis=1)
    rows = pl.ds(experts[0] * D_FF_EXPERT, len(experts) * D_FF_EXPERT)
    return jnp.dot(h, wd_ref[rows, :], preferred_element_type=f32)


def _moe_kernel(x_ref, shift_ref, scale_ref, gate_ref, wt_ref, wtt_ref, wg_ref, wu_ref, wd_ref,
                g_ref, b_ref, o_ref):
    n_sub = x_ref.shape[1] // T
    x = x_ref[0]
    v = _modulate(x_ref, shift_ref, scale_ref).astype(bf16)
    wt = wt_ref[0]
    wt_t = wtt_ref[0]

    member_f = jnp.concatenate(
        [jnp.where(jnp.sum(wt_t[g * EXPERTS_PER_GROUP:(g + 1) * EXPERTS_PER_GROUP], axis=0, keepdims=True)
                   > 0.0, 1.0, 0.0) for g in range(N_GROUPS)]
        + [jnp.zeros((SUBLANES - N_GROUPS, n_sub * T), f32)], axis=0)
    earlier = (lax.broadcasted_iota(i32, (T, T), 0) < lax.broadcasted_iota(i32, (T, T), 1))
    earlier_b = jnp.where(earlier, 1.0, 0.0).astype(bf16)
    subs = [slice(n * T, (n + 1) * T) for n in range(n_sub)]
    rank = [jnp.dot(member_f[:, sub].astype(bf16), earlier_b, preferred_element_type=f32)
            for sub in subs]
    count = functools.reduce(jnp.maximum, [jnp.max(r + member_f[:, sub]) for r, sub in zip(rank, subs)])

    def grouped():
        slot = lax.broadcasted_iota(i32, (GROUP_CAP, T), 0).astype(f32)
        back = (((0,), (0,)), ((), ()))
        y = [jnp.zeros((T, D_MODEL), f32) for _ in subs]
        for g in range(N_GROUPS):
            experts = list(range(g * EXPERTS_PER_GROUP, (g + 1) * EXPERTS_PER_GROUP))
            picks, rows, weights = [], [], []
            for r, sub in zip(rank, subs):
                pick = jnp.logical_and(slot == r[g:g + 1], member_f[g:g + 1, sub] > 0.5)
                pick_f = jnp.where(pick, 1.0, 0.0)
                picks.append(pick_f.astype(bf16))
                rows.append(jnp.dot(picks[-1], v[sub], preferred_element_type=f32).astype(bf16))
                weights.append(jnp.dot(pick_f, wt[sub, experts[0]:experts[-1] + 1],
                                       preferred_element_type=f32, precision=lax.Precision.HIGHEST))
            v_g = jnp.concatenate(rows, axis=0)
            w_g = jnp.concatenate(weights, axis=0)
            y_g = _expert_mlp(v_g, [w_g[:, k:k + 1] for k in range(EXPERTS_PER_GROUP)], experts,
                              wg_ref, wu_ref, wd_ref)
            y_hi = y_g.astype(bf16)
            y_lo = (y_g - y_hi.astype(f32)).astype(bf16)
            for n, pick_b in enumerate(picks):
                part = slice(n * GROUP_CAP, (n + 1) * GROUP_CAP)
                y[n] = (y[n] + lax.dot_general(pick_b, y_hi[part], back, preferred_element_type=f32)
                        + lax.dot_general(pick_b, y_lo[part], back, preferred_element_type=f32))
        return jnp.concatenate(y, axis=0)

    def dense():
        experts = list(range(N_EXPERTS))
        return _expert_mlp(v, [wt[:, n:n + 1] for n in experts], experts, wg_ref, wu_ref, wd_ref)

    y = lax.cond(count <= GROUP_CAP, grouped, dense)
    z = DEEPNORM_ALPHA * x + (1.0 + gate_ref[0]) * y
    o_ref[0] = _layer_norm(z, g_ref[...], b_ref[...])


def _moe_call(x, shift, scale, gate, wt_t, w_gate, w_up, w_down, ln_g, ln_b):
    bsz, seq, d = x.shape
    mt = MOE_TILE
    tok = lambda w: pl.BlockSpec((1, mt, w), lambda b, i: (b, i, 0))
    mod = pl.BlockSpec((1, 1, d), lambda b, i: (b, 0, 0))
    return pl.pallas_call(
        _moe_kernel, name="moe_experts_ln2",
        grid=(bsz, seq // mt),
        in_specs=[tok(d), mod, mod, mod, tok(N_EXPERTS),
                  pl.BlockSpec((1, N_EXPERTS, mt), lambda b, i: (b, 0, i)),
                  _const_spec(w_gate.shape), _const_spec(w_up.shape), _const_spec(w_down.shape),
                  _const_spec((1, d)), _const_spec((1, d))],
        out_specs=tok(d),
        out_shape=jax.ShapeDtypeStruct((bsz, seq, d), f32),
        compiler_params=_params(2),
    )(x, shift, scale, gate, jnp.swapaxes(wt_t, 1, 2), wt_t, w_gate, w_up, w_down,
      ln_g.reshape(1, d), ln_b.reshape(1, d))


def _t5_bucket(n):
    max_exact = REL_BUCKETS // 2
    nf = jnp.maximum(n, 1).astype(f32)
    log_ratio = jnp.log(nf / max_exact) / math.log(REL_MAX_DIST / max_exact)
    large = max_exact + (log_ratio * (REL_BUCKETS - max_exact)).astype(i32)
    large = jnp.minimum(large, REL_BUCKETS - 1)
    return jnp.where(n < max_exact, n, large)


def _toeplitz_kernel(v_ref, o_ref):
    rows, width = o_ref.shape[1], o_ref.shape[2]
    x = jnp.broadcast_to(v_ref[0], (rows, width))
    o_ref[0] = pltpu.roll(x, 0, 1, stride=1, stride_axis=0)


def _toeplitz_call(vec, rows):
    heads, width = vec.shape
    return pl.pallas_call(
        _toeplitz_kernel, name="bias_toeplitz",
        grid=(heads,),
        in_specs=[pl.BlockSpec((1, 1, width), lambda h: (h, 0, 0))],
        out_specs=pl.BlockSpec((1, rows, width), lambda h: (h, 0, 0)),
        out_shape=jax.ShapeDtypeStruct((heads, rows, width), f32),
        compiler_params=_params(1),
    )(vec.reshape(heads, 1, width))


def _bias_tables(rel_table, seq):
    n_cols = (N_FAR + 1) * T
    n_dist = max(seq, n_cols)
    by_dist = rel_table.astype(f32)[_t5_bucket(jnp.arange(n_dist, dtype=i32))].T
    dense_heads = np.r_[A_BIAS0:A_BIAS0 + A_HEADS, D_BIAS0:D_BIAS0 + D_HEADS]
    dense = _toeplitz_call(by_dist[dense_heads, :n_cols] * LOG2E, T)
    dense = dense.reshape(len(dense_heads), T, N_FAR + 1, T).transpose(0, 2, 1, 3)
    bias_a = dense[:A_HEADS]
    bias_d = dense[A_HEADS:]
    width = 4 * BAND

    def band_vec(h0, nh, max_dist, dilation):
        vals = by_dist[h0:h0 + nh, 0:(BAND + 1) * dilation:dilation]
        vals = jnp.where(np.arange(BAND + 1) <= max_dist, vals, NEG_MASK)[:, ::-1]
        return jnp.pad(vals, ((0, 0), (BAND - 1, width - 2 * BAND)), constant_values=NEG_MASK)

    vecs = [band_vec(B_BIAS0, B_QHEADS, B_WINDOW - 1, 1)]
    vecs += [band_vec(C_BIAS0 + g * C_HEADS, C_HEADS, window // dil, dil)
             for g, (window, dil) in enumerate(C_PATTERNS)]
    band = _toeplitz_call(jnp.concatenate(vecs, axis=0), BAND)[:, :, BAND - 1:3 * BAND - 1]
    bias_b = band[:B_QHEADS]
    bias_c = [band[B_QHEADS + g * C_HEADS:B_QHEADS + (g + 1) * C_HEADS] for g in range(len(C_PATTERNS))]
    return bias_a, bias_b, bias_c, bias_d


def _split_cols(w):
    parts, off = {}, 0
    for name, width in IN_SPLITS:
        parts[name] = w[:, off:off + width]
        off += width
    return parts


def _dilate(a, r):
    if r == 1:
        return a
    b, s, f = a.shape
    return a.reshape(b, s // r, r, f).transpose(0, 2, 1, 3).reshape(b * r, s // r, f)


def _undilate(a, r, bsz):
    if r == 1:
        return a
    _, sn, f = a.shape
    return a.reshape(bsz, r, sn, f).transpose(0, 2, 1, 3).reshape(bsz, sn * r, f)


def _layer(l, x, mod, tables, w_router, b_router, w_in, lam_q, lam_k, subln_g, sinks, w_branch, w_out,
           ln1_g, ln1_b, w_exp_gate, w_exp_up, w_exp_down, ln2_g, ln2_b):
    bsz, seq, d = x.shape
    shift_a, scale_a, gate_a, shift_f, scale_f, gate_f = mod
    bias_a, bias_b, bias_c, bias_d = tables
    w = _split_cols(w_in.astype(bf16))
    hw = C_HEADS * HEAD_DIM

    aq, ak, av = _proj_call("proj_a", x, shift_a, scale_a, [w['a_q'], w['a_k'], w['a_v']],
                            [('feat', A_QK ** -0.5 * LOG2E), ('tokh', 2 * A_HEADS, A_QK), ('feat',)],
                            [bf16] * 3)
    o_a = _diff_attn_call(aq, ak, av, bias_a, lam_q, lam_k, subln_g, l)

    bq, bk, bv = _proj_call("proj_b", x, shift_a, scale_a, [w['b_q'], w['b_k'], w['b_v']],
                            [('tok',)] * 3, [bf16] * 3)
    o_b = _band_call("mixer_b_window_attn", bq, bk, bv, bias_b, sinks)

    c_cols = [w[n][:, g * hw:(g + 1) * hw] for g in range(len(C_PATTERNS)) for n in ('c_q', 'c_k', 'c_v')]
    c_proj = _proj_call("proj_c", x, shift_a, scale_a, c_cols, [('tok',)] * 9, [bf16] * 9)
    o_c, lse_c = [], []
    for g, (_, dil) in enumerate(C_PATTERNS):
        qg, kg, vg = (_dilate(a, dil) for a in c_proj[3 * g:3 * g + 3])
        og, lg = _band_call(f"mixer_c_dilated_attn_{g}", qg, kg, vg, bias_c[g], None)
        o_c.append(_undilate(og, dil, bsz))
        lse_c.append(_undilate(lg, dil, bsz))

    dq, dk, dv, iq, ik, iw = _proj_call(
        "proj_d", x, shift_a, scale_a, [w['d_q'], w['d_k'], w['d_v'], w['i_q'], w['i_k'], w['i_w']],
        [('feat', HEAD_DIM ** -0.5 * LOG2E), ('tok',), ('feat',), ('feat',), ('tok',), ('feat',)],
        [bf16] * 5 + [f32])
    o_d = _sparse_attn_call(dq, dk, dv, iq, ik, iw, bias_d)

    x1 = _merge_call(x, shift_a, scale_a, gate_a, o_a, o_b, o_c, lse_c, o_d, w['gate'],
                     w_branch.astype(bf16), w_out.astype(bf16), ln1_g, ln1_b)

    wt = _router_call(x1, shift_f, scale_f, w_router, b_router)
    wd_all = w_exp_down.reshape(N_EXPERTS * D_FF_EXPERT, d).astype(bf16)
    return _moe_call(x1, shift_f, scale_f, gate_f, wt, w_exp_gate.astype(bf16), w_exp_up.astype(bf16),
                     wd_all, ln2_g, ln2_b)


def kernel(x, c, rel_table, w_router, b_router, w_ada, b_ada, w_in, a_lambda_q, a_lambda_k, a_subln_g,
           b_sinks, w_branch, w_out, ln1_g, ln1_b, w_exp_gate, w_exp_up, w_exp_down, ln2_g, ln2_b):
    bsz, seq, d = x.shape
    assert d == D_MODEL and seq % (BAND * C_PATTERNS[-1][1]) == 0 and seq % BAND_TILE == 0
    mod = _ada_call(c, w_ada, b_ada)
    tables = _bias_tables(rel_table, seq)
    for l in range(DEPTH):
        mod_l = tuple(m.reshape(bsz, 1, d) for m in jnp.split(mod[l], 6, axis=-1))
        x = _layer(l, x, mod_l, tables, w_router, b_router, w_in[l], a_lambda_q[l], a_lambda_k[l],
                   a_subln_g[l], b_sinks[l], w_branch[l], w_out[l], ln1_g[l], ln1_b[l],
                   w_exp_gate[l], w_exp_up[l], w_exp_down[l], ln2_g[l], ln2_b[l])
    return x
```

```python
import functools
import math

import numpy as np
import jax
import jax.numpy as jnp
from jax import lax
from jax.experimental import pallas as pl
from jax.experimental.pallas import tpu as pltpu

f32 = jnp.float32
bf16 = jnp.bfloat16
i32 = jnp.int32
i16 = jnp.int16

D_MODEL = 1024
DEPTH = 2
HEAD_DIM = 64
A_HEADS = 4
A_QK = 32
A_V = 2 * A_QK
B_QHEADS = 4
B_KVHEADS = 2
B_WINDOW = 128
C_PATTERNS = ((128, 1), (512, 4), (2048, 16))
C_HEADS = 4
D_HEADS = 4
D_TOPK_MAX = 256
IDX_HEADS = 8
IDX_DIM = 32
REL_BUCKETS = 32
REL_MAX_DIST = 2048
A_BIAS0 = 0
B_BIAS0 = A_BIAS0 + A_HEADS
C_BIAS0 = B_BIAS0 + B_QHEADS
D_BIAS0 = C_BIAS0 + len(C_PATTERNS) * C_HEADS
N_BIAS_HEADS = D_BIAS0 + D_HEADS
N_BRANCH = 4
BRANCH_WIDTH = 256
N_EXPERTS = 16
N_GROUPS = 4
EXPERTS_PER_GROUP = N_EXPERTS // N_GROUPS
D_FF_EXPERT = 256
DEEPNORM_ALPHA = (2 * DEPTH) ** 0.25
LN_EPS = 1e-5

IN_SPLITS = (
    ('a_q', A_HEADS * 2 * A_QK), ('a_k', A_HEADS * 2 * A_QK), ('a_v', A_HEADS * A_V),
    ('b_q', B_QHEADS * HEAD_DIM), ('b_k', B_KVHEADS * HEAD_DIM), ('b_v', B_KVHEADS * HEAD_DIM),
    ('c_q', len(C_PATTERNS) * C_HEADS * HEAD_DIM), ('c_k', len(C_PATTERNS) * C_HEADS * HEAD_DIM),
    ('c_v', len(C_PATTERNS) * C_HEADS * HEAD_DIM),
    ('d_q', D_HEADS * HEAD_DIM), ('d_k', HEAD_DIM), ('d_v', HEAD_DIM),
    ('i_q', IDX_HEADS * IDX_DIM), ('i_k', IDX_DIM), ('i_w', IDX_HEADS),
    ('gate', N_BRANCH * D_MODEL),
)

LANES = 128
SUBLANES = 8
PACK = 16
T = 256
BAND = 128
BAND_TILE = 512
PROJ_TILE = 512
VMEM_LIMIT = 56 * 1024 * 1024
N_FAR = -(-(REL_MAX_DIST + T - 1) // T)

KEY_BLOCKS = 4
LOG2E = math.log2(math.e)

NEG_INIT = -1e30
NEG_MASK = -2e30
INT_MIN = -2 ** 31
I16_MIN = -2 ** 15


def _params(n_axes):
    return pltpu.CompilerParams(dimension_semantics=("arbitrary",) * n_axes,
                                vmem_limit_bytes=VMEM_LIMIT)


def _const_spec(shape):
    nd = len(shape)
    return pl.BlockSpec(shape, lambda *_: (0,) * nd, pipeline_mode=pl.Buffered(1))


def _modulate(x_ref, shift_ref, scale_ref):
    return x_ref[0] * (1.0 + scale_ref[0]) + shift_ref[0]


def _layer_norm(z, g, b):
    mu = jnp.mean(z, axis=-1, keepdims=True)
    zc = z - mu
    var = jnp.mean(zc * zc, axis=-1, keepdims=True)
    return zc * lax.rsqrt(var + LN_EPS) * g + b


def _ada_kernel(c_ref, w_ref, b_ref, o_ref):
    c = c_ref[...]
    cond = c * jax.nn.sigmoid(c)
    o_ref[0] = jnp.dot(cond, w_ref[0], preferred_element_type=f32,
                       precision=lax.Precision.HIGHEST) + b_ref[0]


def _ada_call(c, w_ada, b_ada):
    depth, d, n = w_ada.shape
    bsz = c.shape[0]
    rows = -(-bsz // SUBLANES) * SUBLANES
    c_pad = jnp.zeros((rows, d), f32).at[:bsz].set(c)
    tn = 1536
    out = pl.pallas_call(
        _ada_kernel, name="ada_mod",
        grid=(depth, n // tn),
        in_specs=[pl.BlockSpec((rows, d), lambda l, j: (0, 0)),
                  pl.BlockSpec((1, d, tn), lambda l, j: (l, 0, j)),
                  pl.BlockSpec((1, 1, tn), lambda l, j: (l, 0, j))],
        out_specs=pl.BlockSpec((1, rows, tn), lambda l, j: (l, 0, j)),
        out_shape=jax.ShapeDtypeStruct((depth, rows, n), f32),
        compiler_params=_params(2),
    )(c_pad, w_ada, b_ada.reshape(depth, 1, n))
    return out[:, :bsz]


def _proj_kernel(*refs, kinds):
    n = len(kinds)
    x_ref, shift_ref, scale_ref = refs[:3]
    w_refs = refs[3:3 + n]
    o_refs = refs[3 + n:]
    u = _modulate(x_ref, shift_ref, scale_ref).astype(bf16)
    for kind, w_ref, o_ref in zip(kinds, w_refs, o_refs):
        if kind[0] == 'feat':
            r = lax.dot_general(w_ref[...], u, (((1,), (1,)), ((), ())), preferred_element_type=f32)
            if len(kind) > 1:
                r = r * kind[1]
            for c in range(o_ref.shape[1]):
                o_ref[0, c] = r[:, c * T:(c + 1) * T].astype(o_ref.dtype)
        elif kind[0] == 'tok':
            r = jnp.dot(u, w_ref[...], preferred_element_type=f32)
            o_ref[0] = r.astype(o_ref.dtype)
        else:
            _, heads, hd = kind
            r = jnp.dot(u, w_ref[...], preferred_element_type=f32)
            for h in range(heads):
                o_ref[0, h] = r[:, h * hd:(h + 1) * hd].astype(o_ref.dtype)


def _proj_call(name, x, shift, scale, w_cols, kinds, dtypes):
    bsz, seq, d = x.shape
    pt = PROJ_TILE
    weights, w_specs, out_specs, out_shapes = [], [], [], []
    for w, kind, dt in zip(w_cols, kinds, dtypes):
        feat = w.shape[1]
        if kind[0] == 'feat':
            weights.append(w.T)
            w_specs.append(_const_spec((feat, d)))
            out_specs.append(pl.BlockSpec((1, pt // T, feat, T), lambda b, i: (b, i, 0, 0)))
            out_shapes.append(jax.ShapeDtypeStruct((bsz, seq // T, feat, T), dt))
        elif kind[0] == 'tok':
            weights.append(w)
            w_specs.append(_const_spec((d, feat)))
            out_specs.append(pl.BlockSpec((1, pt, feat), lambda b, i: (b, i, 0)))
            out_shapes.append(jax.ShapeDtypeStruct((bsz, seq, feat), dt))
        else:
            _, heads, hd = kind
            weights.append(w)
            w_specs.append(_const_spec((d, feat)))
            out_specs.append(pl.BlockSpec((1, heads, pt, hd), lambda b, i: (b, 0, i, 0)))
            out_shapes.append(jax.ShapeDtypeStruct((bsz, heads, seq, hd), dt))
    mod_spec = pl.BlockSpec((1, 1, d), lambda b, i: (b, 0, 0))
    return pl.pallas_call(
        functools.partial(_proj_kernel, kinds=tuple(kinds)), name=name,
        grid=(bsz, seq // pt),
        in_specs=[pl.BlockSpec((1, pt, d), lambda b, i: (b, i, 0)), mod_spec, mod_spec] + w_specs,
        out_specs=out_specs, out_shape=out_shapes,
        compiler_params=_params(2),
    )(x, shift, scale, *weights)


def _softmax_step(s, v_t, state):
    mx, acc = state
    mx_new = jnp.maximum(mx, jnp.max(s, axis=0, keepdims=True))
    alpha = jnp.exp2(mx - mx_new)
    p = jnp.exp2(s - mx_new).astype(bf16)
    v_ext = jnp.concatenate([v_t, jnp.ones((PACK, v_t.shape[1]), bf16)], axis=0)
    acc = alpha * acc + jnp.dot(v_ext, p, preferred_element_type=f32)
    return mx_new, acc


def _softmax_init(dv):
    return (jnp.full((1, T), NEG_INIT, f32), jnp.zeros((dv + PACK, T), f32))


def _softmax_result(state, dv):
    _, acc = state
    return acc[:dv] / acc[dv:dv + 1]


def _causal_tile():
    key = lax.broadcasted_iota(i32, (T, T), 0)
    qry = lax.broadcasted_iota(i32, (T, T), 1)
    return key <= qry


def _sweep_key_tiles(i, raw_logits, mask_logits, tile_values, state, buf_a, buf_b):
    n_heads = len(state)
    n_full = i // KEY_BLOCKS

    def tile_of(g, t):
        j = g * KEY_BLOCKS + t
        return j, jnp.minimum(j, i)

    def fill(buf, g):
        for t in range(KEY_BLOCKS):
            for h, s in enumerate(raw_logits(tile_of(g, t)[1])):
                buf[t * n_heads + h] = s

    def drain(buf, g, st, tail):
        for t in range(KEY_BLOCKS):
            j, jc = tile_of(g, t)
            v_t = tile_values(jc)
            logits = mask_logits(j, jc, [buf[t * n_heads + h] for h in range(n_heads)], tail)
            st = tuple(_softmax_step(s, v_t, st_h) for s, st_h in zip(logits, st))
        return st

    def pair(gp, st):
        g = 2 * gp
        fill(buf_b, g + 1)
        st = drain(buf_a, g, st, False)
        fill(buf_a, g + 2)
        return drain(buf_b, g + 1, st, False)

    fill(buf_a, 0)
    state = lax.fori_loop(0, n_full // 2, pair, state)
    g_last = n_full - n_full % 2

    def one_more(st):
        fill(buf_b, g_last + 1)
        st = drain(buf_a, g_last, st, False)
        return drain(buf_b, g_last + 1, st, True)

    return lax.cond(n_full % 2 == 1, one_more, lambda st: drain(buf_a, g_last, st, True), state)


def _logit_buffers(n_heads):
    return [pltpu.VMEM((KEY_BLOCKS * n_heads, T, T), f32)] * 2


def _diff_attn_kernel(lq_ref, lk_ref, q_ref, k_ref, v_ref, bias_ref, g_ref, o_ref, buf_a, buf_b,
                      *, lam_init):
    i = pl.program_id(2)
    q_t = q_ref[0, 0]
    q_parts = (q_t[:A_QK], q_t[A_QK:])
    key_minus_query = lax.broadcasted_iota(i32, (T, T), 0) - lax.broadcasted_iota(i32, (T, T), 1)

    def raw_logits(j):
        start = pl.multiple_of(j * T, T)
        bias = bias_ref[0, jnp.minimum(i - j, N_FAR)]
        return [jnp.dot(k_ref[0, m, pl.ds(start, T), :], q_parts[m], preferred_element_type=f32) + bias
                for m in range(2)]

    def mask_logits(j, jc, logits, tail):
        if not tail:
            return logits
        keep = key_minus_query <= (i - j) * T
        return [jnp.where(keep, s, NEG_MASK) for s in logits]

    state = _sweep_key_tiles(i, raw_logits, mask_logits, lambda j: v_ref[0, j],
                             (_softmax_init(A_V), _softmax_init(A_V)), buf_a, buf_b)
    lq = lq_ref[...]
    lk = lk_ref[...]
    lam = (jnp.exp(jnp.sum(lq[0:1] * lk[0:1], axis=1, keepdims=True))
           - jnp.exp(jnp.sum(lq[1:2] * lk[1:2], axis=1, keepdims=True)) + lam_init)
    o = _softmax_result(state[0], A_V) - lam * _softmax_result(state[1], A_V)
    ms = jnp.mean(o * o, axis=0, keepdims=True)
    o = o * lax.rsqrt(ms + LN_EPS) * g_ref[...] * (1.0 - lam_init)
    o_ref[0, 0] = o.astype(o_ref.dtype)


def _diff_attn_call(q_t, k_h, v_t, bias, lam_q, lam_k, subln_g, layer_idx):
    bsz, nt, _, _ = q_t.shape
    seq = nt * T
    lam_init = 0.8 - 0.6 * math.exp(-0.3 * layer_idx)
    g = jnp.broadcast_to(subln_g.astype(f32)[:, None], (A_V, T))
    nb = bias.shape[1]
    return pl.pallas_call(
        functools.partial(_diff_attn_kernel, lam_init=lam_init), name="mixer_a_diff_attn",
        grid=(bsz, A_HEADS, nt),
        in_specs=[_const_spec((2, A_QK)), _const_spec((2, A_QK)),
                  pl.BlockSpec((1, 1, 2 * A_QK, T), lambda b, h, i: (b, i, h, 0)),
                  pl.BlockSpec((1, 2, seq, A_QK), lambda b, h, i: (b, h, 0, 0)),
                  pl.BlockSpec((1, nt, A_V, T), lambda b, h, i: (b, 0, h, 0)),
                  pl.BlockSpec((1, nb, T, T), lambda b, h, i: (h, 0, 0, 0)),
                  _const_spec((A_V, T))],
        out_specs=pl.BlockSpec((1, 1, A_V, T), lambda b, h, i: (b, i, h, 0)),
        out_shape=jax.ShapeDtypeStruct((bsz, nt, A_HEADS * A_V, T), bf16),
        scratch_shapes=_logit_buffers(2),
        compiler_params=_params(3),
    )(lam_q.astype(f32), lam_k.astype(f32), q_t, k_h, v_t, bias, g)


def _sparse_attn_kernel(q_ref, k_ref, v_ref, iq_ref, ik_ref, iw_ref, bias_ref, o_ref,
                        key_ref, hi_ref, lo_ref, buf_a, buf_b, *, k_sel, seq):
    i = pl.program_id(1)
    causal = _causal_tile()
    key_iota = lax.broadcasted_iota(i32, (T, T), 0)
    iq_t = iq_ref[0, 0]
    iw_t = iw_ref[0, 0]

    def score_block(j, diagonal):
        start = pl.multiple_of(j * T, T)
        ikb = ik_ref[0, pl.ds(start, T), :]
        sc = jnp.zeros((T, T), f32)
        for h in range(IDX_HEADS):
            r = jnp.dot(ikb, iq_t[h * IDX_DIM:(h + 1) * IDX_DIM], preferred_element_type=f32)
            sc = sc + jnp.maximum(r, 0.0) * iw_t[h:h + 1]
        bits = pltpu.bitcast(sc, i32)
        key = bits ^ ((bits >> 31) & jnp.int32(0x7FFFFFFF))
        if diagonal:
            key = jnp.where(causal, key, jnp.int32(INT_MIN))
        key_ref[pl.ds(start, T), :] = key
        hi_ref[pl.ds(start, T), :] = (key >> 16).astype(i16)
        lo_ref[pl.ds(start, T), :] = (((key ^ jnp.int32(0x8000)) << 16) >> 16).astype(i16)

    def _score_pair(jj, carry):
        score_block(2 * jj, False)
        score_block(2 * jj + 1, False)
        return carry

    lax.fori_loop(0, i // 2, _score_pair, 0)

    @pl.when(i % 2 == 1)
    def _():
        score_block(i - 1, False)

    score_block(i, True)
    after = pl.multiple_of((i + 1) * T, T)
    hi_ref[pl.ds(after, T), :] = jnp.full((T, T), I16_MIN, i16)
    lo_ref[pl.ds(after, T), :] = jnp.full((T, T), I16_MIN, i16)
    n_chunks = (i + 2) // 2

    def count16(ref, cand):
        def body(c, acc):
            start = pl.multiple_of(c * 2 * T, 2 * T)
            ind = jnp.where(ref[pl.ds(start, 2 * T), :] >= cand, jnp.int16(1), jnp.int16(0))
            for r in range(2 * T // PACK):
                acc = acc + ind[r * PACK:(r + 1) * PACK]
            return acc
        acc = lax.fori_loop(0, n_chunks, body, jnp.zeros((PACK, T), i16))
        return acc.astype(i32).sum(axis=0, keepdims=True)

    def search16(ref, n_start):
        def bit_step(b, carry):
            t_u, n_ge = carry
            cand_u = t_u | lax.shift_left(jnp.int32(1), 15 - b)
            cnt = count16(ref, (cand_u + I16_MIN).astype(i16))
            take = cnt >= k_sel
            return jnp.where(take, cand_u, t_u), jnp.where(take, cnt, n_ge)

        return lax.fori_loop(0, 16, bit_step, (jnp.zeros((1, T), i32), n_start))

    def any_above_k(n):
        return jnp.max(n.astype(f32)) > k_sel

    n_valid = i * T + lax.broadcasted_iota(i32, (1, T), 1) + 1
    hi_u, n_hi = search16(hi_ref, n_valid)

    def refine():
        p16 = (hi_u + I16_MIN).astype(i16)

        def body(c, carry):
            rows = pl.ds(pl.multiple_of(c * 2 * T, 2 * T), 2 * T)
            hi = hi_ref[rows, :]
            lo_ref[rows, :] = jnp.where(hi > p16, jnp.int16(-I16_MIN - 1),
                                        jnp.where(hi == p16, lo_ref[rows, :], jnp.int16(I16_MIN)))
            return carry

        lax.fori_loop(0, n_chunks, body, 0)
        return search16(lo_ref, n_hi)

    lo_u, n_ge = lax.cond(any_above_k(n_hi), refine, lambda: (jnp.zeros((1, T), i32), n_hi))
    thr = jnp.maximum(((hi_u + I16_MIN) << 16) + lo_u, jnp.int32(INT_MIN + 1))

    tied = n_ge > k_sel

    def count(pred):
        def body(j, acc):
            start = pl.multiple_of(j * T, T)
            ind = jnp.where(pred(key_ref[pl.ds(start, T), :], start), 1, 0).astype(i32)
            return acc + ind.reshape(T // SUBLANES, SUBLANES, T).sum(axis=0)
        acc = lax.fori_loop(0, i + 1, body, jnp.zeros((SUBLANES, T), i32))
        return acc.sum(axis=0, keepdims=True)

    def tie_limit():
        need = k_sel - count(lambda kb, _: kb > thr)
        n_bits = max(1, (seq - 1).bit_length())

        def idx_step(b, lim):
            bit = lax.shift_left(jnp.int32(1), n_bits - 1 - b)
            probe = lim + bit - 1
            cnt = count(lambda kb, start: jnp.logical_and(kb == thr, key_iota + start <= probe))
            return jnp.where(cnt < need, lim + bit, lim)

        lim = lax.fori_loop(0, n_bits, idx_step, jnp.zeros((1, T), i32))
        return jnp.where(tied, lim, jnp.int32(seq))

    q_t = q_ref[0, 0]

    def attend(idx_lim):
        def raw_logits(j):
            kb = k_ref[0, pl.ds(pl.multiple_of(j * T, T), T), :]
            return [jnp.dot(kb, q_t[h * HEAD_DIM:(h + 1) * HEAD_DIM], preferred_element_type=f32)
                    + bias_ref[h, jnp.minimum(i - j, N_FAR)] for h in range(D_HEADS)]

        def mask_logits(j, jc, logits, tail):
            start = pl.multiple_of(jc * T, T)
            keys = key_ref[pl.ds(start, T), :]
            drop = keys < thr
            if idx_lim is not None:
                drop = jnp.logical_or(drop, jnp.logical_and(keys == thr, key_iota + start > idx_lim))
            if tail:
                drop = jnp.logical_or(drop, j > i)
            return [jnp.where(drop, NEG_MASK, s) for s in logits]

        return _sweep_key_tiles(i, raw_logits, mask_logits, lambda j: v_ref[0, j],
                                tuple(_softmax_init(HEAD_DIM) for _ in range(D_HEADS)), buf_a, buf_b)

    state = lax.cond(any_above_k(n_ge), lambda: attend(tie_limit()), lambda: attend(None))
    for h in range(D_HEADS):
        o_ref[0, 0, h * HEAD_DIM:(h + 1) * HEAD_DIM, :] = (
            _softmax_result(state[h], HEAD_DIM).astype(o_ref.dtype))


def _sparse_attn_call(q_t, k, v_t, iq_t, ik, iw_t, bias):
    bsz, nt, _, _ = q_t.shape
    seq = nt * T
    assert seq // PACK < -I16_MIN
    k_sel = min(D_TOPK_MAX, seq // 4)
    tile = lambda f: pl.BlockSpec((1, 1, f, T), lambda b, i: (b, i, 0, 0))
    whole = lambda shape: pl.BlockSpec((1,) + shape, lambda b, i: (b,) + (0,) * len(shape))
    return pl.pallas_call(
        functools.partial(_sparse_attn_kernel, k_sel=k_sel, seq=seq), name="mixer_d_sparse_attn",
        grid=(bsz, nt),
        in_specs=[tile(D_HEADS * HEAD_DIM), whole((seq, HEAD_DIM)), whole((nt, HEAD_DIM, T)),
                  tile(IDX_HEADS * IDX_DIM), whole((seq, IDX_DIM)), tile(IDX_HEADS),
                  _const_spec(bias.shape)],
        out_specs=tile(D_HEADS * HEAD_DIM),
        out_shape=jax.ShapeDtypeStruct((bsz, nt, D_HEADS * HEAD_DIM, T), bf16),
        scratch_shapes=[pltpu.VMEM((seq, T), i32), pltpu.VMEM((seq + T, T), i16),
                        pltpu.VMEM((seq + T, T), i16)] + _logit_buffers(D_HEADS),
        compiler_params=_params(2),
    )(q_t, k, v_t, iq_t, ik, iw_t, bias)


def _band_kernel(*refs, hq, hk, has_sink, tb):
    if has_sink:
        sink_ref, q_ref, kp_ref, kc_ref, vp_ref, vc_ref, bias_ref, o_ref = refs
    else:
        q_ref, kp_ref, kc_ref, vp_ref, vc_ref, bias_ref, o_ref, lse_ref = refs
    i = pl.program_id(1)
    rep = hq // hk
    from_prev_tile = lax.broadcasted_iota(i32, (BAND, 2 * BAND), 1) < BAND
    contract_last = (((1,), (1,)), ((), ()))
    n_sub = tb // BAND
    vals, logits = [], []
    for a in range(n_sub):
        if a == 0:
            kk = jnp.concatenate([kp_ref[0], kc_ref[0, 0:BAND, :]], axis=0)
            vv = jnp.concatenate([vp_ref[0], vc_ref[0, 0:BAND, :]], axis=0)
        else:
            kk = kc_ref[0, (a - 1) * BAND:(a + 1) * BAND, :]
            vv = vc_ref[0, (a - 1) * BAND:(a + 1) * BAND, :]
        qa = q_ref[0, a * BAND:(a + 1) * BAND, :]
        row = []
        for h in range(hq):
            g = h // rep
            hs = slice(h * HEAD_DIM, (h + 1) * HEAD_DIM)
            gs = slice(g * HEAD_DIM, (g + 1) * HEAD_DIM)
            s = lax.dot_general(qa[:, hs], kk[:, gs], contract_last, preferred_element_type=f32)
            s = s * (HEAD_DIM ** -0.5) + bias_ref[h]
            if a == 0:
                s = jnp.where(jnp.logical_and(from_prev_tile, i == 0), NEG_MASK, s)
            row.append(s)
        logits.append(row)
        vals.append(vv)
    for a in range(n_sub):
        outs, lses = [], []
        for h in range(hq):
            g = h // rep
            gs = slice(g * HEAD_DIM, (g + 1) * HEAD_DIM)
            s = logits[a][h]
            m = jnp.max(s, axis=1, keepdims=True)
            if has_sink:
                m = jnp.maximum(m, sink_ref[h])
            e = jnp.exp(s - m)
            den = jnp.sum(e, axis=1, keepdims=True)
            if has_sink:
                den = den + jnp.exp(sink_ref[h] - m)
            o = jnp.dot(e.astype(bf16), vals[a][:, gs], preferred_element_type=f32) / den
            outs.append(o)
            if not has_sink:
                lses.append(m + jnp.log(den))
        rows = slice(a * BAND, (a + 1) * BAND)
        o_ref[0, rows, :] = jnp.concatenate(outs, axis=1).astype(o_ref.dtype)
        if not has_sink:
            lse_ref[0, rows, :] = jnp.concatenate(lses, axis=1)


def _band_call(name, q, k, v, bias, sinks):
    bn, sn, qw = q.shape
    kw = k.shape[2]
    hq, hk = qw // HEAD_DIM, kw // HEAD_DIM
    tb = min(BAND_TILE, sn)
    per = tb // BAND
    has_sink = sinks is not None
    cur = lambda w: pl.BlockSpec((1, tb, w), lambda b, i: (b, i, 0))
    prev = lambda w: pl.BlockSpec((1, BAND, w), lambda b, i: (b, jnp.maximum(i * per - 1, 0), 0))
    in_specs = [cur(qw), prev(kw), cur(kw), prev(kw), cur(kw), _const_spec(bias.shape)]
    args = [q, k, k, v, v, bias]
    if has_sink:
        in_specs = [pl.BlockSpec(memory_space=pltpu.SMEM)] + in_specs
        args = [sinks.astype(f32)] + args
        out_specs = cur(qw)
        out_shape = jax.ShapeDtypeStruct((bn, sn, qw), bf16)
    else:
        out_specs = [cur(qw), cur(hq)]
        out_shape = [jax.ShapeDtypeStruct((bn, sn, qw), f32), jax.ShapeDtypeStruct((bn, sn, hq), f32)]
    return pl.pallas_call(
        functools.partial(_band_kernel, hq=hq, hk=hk, has_sink=has_sink, tb=tb), name=name,
        grid=(bn, sn // tb), in_specs=in_specs, out_specs=out_specs, out_shape=out_shape,
        compiler_params=_params(2),
    )(*args)


def _merge_kernel(x_ref, shift_ref, scale_ref, gate_ref, oa_ref, ob_ref, oc_refs, lse_refs, od_ref,
                  wg_ref, wb_ref, wo_ref, g_ref, b_ref, o_ref):
    x = x_ref[0]
    u = _modulate(x_ref, shift_ref, scale_ref).astype(bf16)
    lse = [r[0] for r in lse_refs]
    top = jnp.maximum(jnp.maximum(lse[0], lse[1]), lse[2])
    ws = [jnp.exp(v - top) for v in lse]
    ws = [jnp.concatenate([jnp.broadcast_to(w[:, h:h + 1], (T, HEAD_DIM)) for h in range(C_HEADS)], axis=1)
          for w in ws]
    o_c = (ws[0] * oc_refs[0][0] + ws[1] * oc_refs[1][0] + ws[2] * oc_refs[2][0]) / (ws[0] + ws[1] + ws[2])
    contract_first = (((0,), (0,)), ((), ()))
    branch = (
        lax.dot_general(oa_ref[0, 0], wb_ref[0], contract_first, preferred_element_type=f32),
        jnp.dot(ob_ref[0], wb_ref[1], preferred_element_type=f32),
        jnp.dot(o_c.astype(bf16), wb_ref[2], preferred_element_type=f32),
        lax.dot_general(od_ref[0, 0], wb_ref[3], contract_first, preferred_element_type=f32),
    )
    merged = None
    for n in range(N_BRANCH):
        logits = jnp.dot(u, wg_ref[:, n * D_MODEL:(n + 1) * D_MODEL], preferred_element_type=f32)
        term = jax.nn.sigmoid(logits) * branch[n]
        merged = term if merged is None else merged + term
    y = jnp.dot(merged.astype(bf16), wo_ref[...], preferred_element_type=f32)
    z = DEEPNORM_ALPHA * x + (1.0 + gate_ref[0]) * y
    o_ref[0] = _layer_norm(z, g_ref[...], b_ref[...])


def _merge_call(x, shift, scale, gate, o_a, o_b, o_c, lse_c, o_d, w_gate, w_branch, w_out, ln_g, ln_b):
    bsz, seq, d = x.shape
    nt = seq // T
    tok = lambda w: pl.BlockSpec((1, T, w), lambda b, i: (b, i, 0))
    feat = lambda w: pl.BlockSpec((1, 1, w, T), lambda b, i: (b, i, 0, 0))
    mod = pl.BlockSpec((1, 1, d), lambda b, i: (b, 0, 0))

    def kernel(*refs):
        (x_ref, s_ref, c_ref, g_ref, oa, ob, c0, c1, c2, l0, l1, l2, od, wg, wb, wo, lg, lb, o) = refs
        _merge_kernel(x_ref, s_ref, c_ref, g_ref, oa, ob, (c0, c1, c2), (l0, l1, l2), od, wg, wb, wo, lg, lb, o)

    return pl.pallas_call(
        kernel, name="merge_outproj_ln1",
        grid=(bsz, nt),
        in_specs=[tok(d), mod, mod, mod, feat(BRANCH_WIDTH), tok(BRANCH_WIDTH)]
                 + [tok(BRANCH_WIDTH)] * 3 + [tok(C_HEADS)] * 3 + [feat(BRANCH_WIDTH),
                 _const_spec(w_gate.shape), _const_spec(w_branch.shape), _const_spec(w_out.shape),
                 _const_spec((1, d)), _const_spec((1, d))],
        out_specs=tok(d),
        out_shape=jax.ShapeDtypeStruct((bsz, seq, d), f32),
        compiler_params=_params(2),
    )(x, shift, scale, gate, o_a, o_b, *o_c, *lse_c, o_d, w_gate, w_branch, w_out,
      ln_g.reshape(1, d), ln_b.reshape(1, d))


ROUTER_TILE = 512


def _router_kernel(x_ref, shift_ref, scale_ref, wr_ref, br_ref, o_ref):
    v = _modulate(x_ref, shift_ref, scale_ref)
    logits = lax.dot_general(wr_ref[...], v, (((1,), (1,)), ((), ())), preferred_element_type=f32,
                             precision=lax.Precision.HIGHEST)
    e = jnp.exp(logits - jnp.max(logits, axis=0, keepdims=True))
    probs = e / jnp.sum(e, axis=0, keepdims=True)
    score = probs + br_ref[...]
    p = [probs[n:n + 1] for n in range(N_EXPERTS)]
    s = [score[n:n + 1] for n in range(N_EXPERTS)]
    best_g = None
    for g in range(N_GROUPS):
        a, b, c, d = s[g * EXPERTS_PER_GROUP:(g + 1) * EXPERTS_PER_GROUP]
        hi1, lo1, hi2, lo2 = jnp.maximum(a, b), jnp.minimum(a, b), jnp.maximum(c, d), jnp.minimum(c, d)
        gs = jnp.maximum(hi1, hi2) + jnp.maximum(jnp.minimum(hi1, hi2), jnp.maximum(lo1, lo2))
        if best_g is None:
            best_g, grp = gs, jnp.zeros_like(gs, dtype=i32)
        else:
            better = gs > best_g
            best_g = jnp.where(better, gs, best_g)
            grp = jnp.where(better, g, grp)
    neg_inf = jnp.float32(-jnp.inf)
    cand = [jnp.where(grp == n // EXPERTS_PER_GROUP, s[n], neg_inf) for n in range(N_EXPERTS)]

    def first_argmax(vals):
        best, idx = vals[0], jnp.zeros_like(grp)
        for n in range(1, N_EXPERTS):
            better = vals[n] > best
            best = jnp.where(better, vals[n], best)
            idx = jnp.where(better, n, idx)
        return idx

    i1 = first_argmax(cand)
    i2 = first_argmax([jnp.where(i1 == n, neg_inf, cand[n]) for n in range(N_EXPERTS)])
    w1 = sum(jnp.where(i1 == n, p[n], 0.0) for n in range(N_EXPERTS))
    w2 = sum(jnp.where(i2 == n, p[n], 0.0) for n in range(N_EXPERTS))
    tot = w1 + w2
    rows = [jnp.where(i1 == n, w1 / tot, jnp.where(i2 == n, w2 / tot, 0.0)) for n in range(N_EXPERTS)]
    o_ref[0] = jnp.concatenate(rows, axis=0)


def _router_call(x, shift, scale, w_router, b_router):
    bsz, seq, d = x.shape
    rt = min(ROUTER_TILE, seq)
    mod = pl.BlockSpec((1, 1, d), lambda b, i: (b, 0, 0))
    return pl.pallas_call(
        _router_kernel, name="moe_router",
        grid=(bsz, seq // rt),
        in_specs=[pl.BlockSpec((1, rt, d), lambda b, i: (b, i, 0)), mod, mod,
                  _const_spec((N_EXPERTS, d)), _const_spec((N_EXPERTS, 1))],
        out_specs=pl.BlockSpec((1, N_EXPERTS, rt), lambda b, i: (b, 0, i)),
        out_shape=jax.ShapeDtypeStruct((bsz, N_EXPERTS, seq), f32),
        compiler_params=_params(2),
    )(x, shift, scale, w_router.T.astype(f32), b_router.astype(f32).reshape(N_EXPERTS, 1))


GROUP_CAP = 96
MOE_TILE = 512


def _expert_mlp(v, w_cols, experts, wg_ref, wu_ref, wd_ref):
    parts = []
    for w_n, n in zip(w_cols, experts):
        hg = jnp.dot(v, wg_ref[n], preferred_element_type=f32)
        hu = jnp.dot(v, wu_ref[n], preferred_element_type=f32)
        act = hg * jax.nn.sigmoid(hg) * hu
        parts.append(jnp.where(w_n != 0.0, act * w_n, 0.0).astype(bf16))
    h = jnp.concatenate(parts, axis=1)
    rows = pl.ds(experts[0] * D_FF_EXPERT, len(experts) * D_FF_EXPERT)
    return jnp.dot(h, wd_ref[rows, :], preferred_element_type=f32)


def _moe_kernel(x_ref, shift_ref, scale_ref, gate_ref, wt_ref, wtt_ref, wg_ref, wu_ref, wd_ref,
                g_ref, b_ref, o_ref):
    n_sub = x_ref.shape[1] // T
    x = x_ref[0]
    v = _modulate(x_ref, shift_ref, scale_ref).astype(bf16)
    wt = wt_ref[0]
    wt_t = wtt_ref[0]

    member_f = jnp.concatenate(
        [jnp.where(jnp.sum(wt_t[g * EXPERTS_PER_GROUP:(g + 1) * EXPERTS_PER_GROUP], axis=0, keepdims=True)
                   > 0.0, 1.0, 0.0) for g in range(N_GROUPS)]
        + [jnp.zeros((SUBLANES - N_GROUPS, n_sub * T), f32)], axis=0)
    earlier = (lax.broadcasted_iota(i32, (T, T), 0) < lax.broadcasted_iota(i32, (T, T), 1))
    earlier_b = jnp.where(earlier, 1.0, 0.0).astype(bf16)
    subs = [slice(n * T, (n + 1) * T) for n in range(n_sub)]
    rank = [jnp.dot(member_f[:, sub].astype(bf16), earlier_b, preferred_element_type=f32)
            for sub in subs]
    count = functools.reduce(jnp.maximum, [jnp.max(r + member_f[:, sub]) for r, sub in zip(rank, subs)])

    def grouped():
        slot = lax.broadcasted_iota(i32, (GROUP_CAP, T), 0).astype(f32)
        back = (((0,), (0,)), ((), ()))
        y = [jnp.zeros((T, D_MODEL), f32) for _ in subs]
        for g in range(N_GROUPS):
            experts = list(range(g * EXPERTS_PER_GROUP, (g + 1) * EXPERTS_PER_GROUP))
            picks, rows, weights = [], [], []
            for r, sub in zip(rank, subs):
                pick = jnp.logical_and(slot == r[g:g + 1], member_f[g:g + 1, sub] > 0.5)
                pick_f = jnp.where(pick, 1.0, 0.0)
                picks.append(pick_f.astype(bf16))
                rows.append(jnp.dot(picks[-1], v[sub], preferred_element_type=f32).astype(bf16))
                weights.append(jnp.dot(pick_f, wt[sub, experts[0]:experts[-1] + 1],
                                       preferred_element_type=f32, precision=lax.Precision.HIGHEST))
            v_g = jnp.concatenate(rows, axis=0)
            w_g = jnp.concatenate(weights, axis=0)
            y_g = _expert_mlp(v_g, [w_g[:, k:k + 1] for k in range(EXPERTS_PER_GROUP)], experts,
                              wg_ref, wu_ref, wd_ref)
            y_hi = y_g.astype(bf16)
            y_lo = (y_g - y_hi.astype(f32)).astype(bf16)
            for n, pick_b in enumerate(picks):
                part = slice(n * GROUP_CAP, (n + 1) * GROUP_CAP)
                y[n] = (y[n] + lax.dot_general(pick_b, y_hi[part], back, preferred_element_type=f32)
                        + lax.dot_general(pick_b, y_lo[part], back, preferred_element_type=f32))
        return jnp.concatenate(y, axis=0)

    def dense():
        experts = list(range(N_EXPERTS))
        return _expert_mlp(v, [wt[:, n:n + 1] for n in experts], experts, wg_ref, wu_ref, wd_ref)

    y = lax.cond(count <= GROUP_CAP, grouped, dense)
    z = DEEPNORM_ALPHA * x + (1.0 + gate_ref[0]) * y
    o_ref[0] = _layer_norm(z, g_ref[...], b_ref[...])


def _moe_call(x, shift, scale, gate, wt_t, w_gate, w_up, w_down, ln_g, ln_b):
    bsz, seq, d = x.shape
    mt = MOE_TILE
    tok = lambda w: pl.BlockSpec((1, mt, w), lambda b, i: (b, i, 0))
    mod = pl.BlockSpec((1, 1, d), lambda b, i: (b, 0, 0))
    return pl.pallas_call(
        _moe_kernel, name="moe_experts_ln2",
        grid=(bsz, seq // mt),
        in_specs=[tok(d), mod, mod, mod, tok(N_EXPERTS),
                  pl.BlockSpec((1, N_EXPERTS, mt), lambda b, i: (b, 0, i)),
                  _const_spec(w_gate.shape), _const_spec(w_up.shape), _const_spec(w_down.shape),
                  _const_spec((1, d)), _const_spec((1, d))],
        out_specs=tok(d),
        out_shape=jax.ShapeDtypeStruct((bsz, seq, d), f32),
        compiler_params=_params(2),
    )(x, shift, scale, gate, jnp.swapaxes(wt_t, 1, 2), wt_t, w_gate, w_up, w_down,
      ln_g.reshape(1, d), ln_b.reshape(1, d))


def _t5_bucket(n):
    max_exact = REL_BUCKETS // 2
    nf = jnp.maximum(n, 1).astype(f32)
    log_ratio = jnp.log(nf / max_exact) / math.log(REL_MAX_DIST / max_exact)
    large = max_exact + (log_ratio * (REL_BUCKETS - max_exact)).astype(i32)
    large = jnp.minimum(large, REL_BUCKETS - 1)
    return jnp.where(n < max_exact, n, large)


def _toeplitz_kernel(v_ref, o_ref):
    rows, width = o_ref.shape[1], o_ref.shape[2]
    x = jnp.broadcast_to(v_ref[0], (rows, width))
    o_ref[0] = pltpu.roll(x, 0, 1, stride=1, stride_axis=0)


def _toeplitz_call(vec, rows):
    heads, width = vec.shape
    return pl.pallas_call(
        _toeplitz_kernel, name="bias_toeplitz",
        grid=(heads,),
        in_specs=[pl.BlockSpec((1, 1, width), lambda h: (h, 0, 0))],
        out_specs=pl.BlockSpec((1, rows, width), lambda h: (h, 0, 0)),
        out_shape=jax.ShapeDtypeStruct((heads, rows, width), f32),
        compiler_params=_params(1),
    )(vec.reshape(heads, 1, width))


def _bias_tables(rel_table, seq):
    n_cols = (N_FAR + 1) * T
    n_dist = max(seq, n_cols)
    by_dist = rel_table.astype(f32)[_t5_bucket(jnp.arange(n_dist, dtype=i32))].T
    dense_heads = np.r_[A_BIAS0:A_BIAS0 + A_HEADS, D_BIAS0:D_BIAS0 + D_HEADS]
    dense = _toeplitz_call(by_dist[dense_heads, :n_cols] * LOG2E, T)
    dense = dense.reshape(len(dense_heads), T, N_FAR + 1, T).transpose(0, 2, 1, 3)
    bias_a = dense[:A_HEADS]
    bias_d = dense[A_HEADS:]
    width = 4 * BAND

    def band_vec(h0, nh, max_dist, dilation):
        vals = by_dist[h0:h0 + nh, 0:(BAND + 1) * dilation:dilation]
        vals = jnp.where(np.arange(BAND + 1) <= max_dist, vals, NEG_MASK)[:, ::-1]
        return jnp.pad(vals, ((0, 0), (BAND - 1, width - 2 * BAND)), constant_values=NEG_MASK)

    vecs = [band_vec(B_BIAS0, B_QHEADS, B_WINDOW - 1, 1)]
    vecs += [band_vec(C_BIAS0 + g * C_HEADS, C_HEADS, window // dil, dil)
             for g, (window, dil) in enumerate(C_PATTERNS)]
    band = _toeplitz_call(jnp.concatenate(vecs, axis=0), BAND)[:, :, BAND - 1:3 * BAND - 1]
    bias_b = band[:B_QHEADS]
    bias_c = [band[B_QHEADS + g * C_HEADS:B_QHEADS + (g + 1) * C_HEADS] for g in range(len(C_PATTERNS))]
    return bias_a, bias_b, bias_c, bias_d


def _split_cols(w):
    parts, off = {}, 0
    for name, width in IN_SPLITS:
        parts[name] = w[:, off:off + width]
        off += width
    return parts


def _dilate(a, r):
    if r == 1:
        return a
    b, s, f = a.shape
    return a.reshape(b, s // r, r, f).transpose(0, 2, 1, 3).reshape(b * r, s // r, f)


def _undilate(a, r, bsz):
    if r == 1:
        return a
    _, sn, f = a.shape
    return a.reshape(bsz, r, sn, f).transpose(0, 2, 1, 3).reshape(bsz, sn * r, f)


def _layer(l, x, mod, tables, w_router, b_router, w_in, lam_q, lam_k, subln_g, sinks, w_branch, w_out,
           ln1_g, ln1_b, w_exp_gate, w_exp_up, w_exp_down, ln2_g, ln2_b):
    bsz, seq, d = x.shape
    shift_a, scale_a, gate_a, shift_f, scale_f, gate_f = mod
    bias_a, bias_b, bias_c, bias_d = tables
    w = _split_cols(w_in.astype(bf16))
    hw = C_HEADS * HEAD_DIM

    aq, ak, av = _proj_call("proj_a", x, shift_a, scale_a, [w['a_q'], w['a_k'], w['a_v']],
                            [('feat', A_QK ** -0.5 * LOG2E), ('tokh', 2 * A_HEADS, A_QK), ('feat',)],
                            [bf16] * 3)
    o_a = _diff_attn_call(aq, ak, av, bias_a, lam_q, lam_k, subln_g, l)

    bq, bk, bv = _proj_call("proj_b", x, shift_a, scale_a, [w['b_q'], w['b_k'], w['b_v']],
                            [('tok',)] * 3, [bf16] * 3)
    o_b = _band_call("mixer_b_window_attn", bq, bk, bv, bias_b, sinks)

    c_cols = [w[n][:, g * hw:(g + 1) * hw] for g in range(len(C_PATTERNS)) for n in ('c_q', 'c_k', 'c_v')]
    c_proj = _proj_call("proj_c", x, shift_a, scale_a, c_cols, [('tok',)] * 9, [bf16] * 9)
    o_c, lse_c = [], []
    for g, (_, dil) in enumerate(C_PATTERNS):
        qg, kg, vg = (_dilate(a, dil) for a in c_proj[3 * g:3 * g + 3])
        og, lg = _band_call(f"mixer_c_dilated_attn_{g}", qg, kg, vg, bias_c[g], None)
        o_c.append(_undilate(og, dil, bsz))
        lse_c.append(_undilate(lg, dil, bsz))

    dq, dk, dv, iq, ik, iw = _proj_call(
        "proj_d", x, shift_a, scale_a, [w['d_q'], w['d_k'], w['d_v'], w['i_q'], w['i_k'], w['i_w']],
        [('feat', HEAD_DIM ** -0.5 * LOG2E), ('tok',), ('feat',), ('feat',), ('tok',), ('feat',)],
        [bf16] * 5 + [f32])
    o_d = _sparse_attn_call(dq, dk, dv, iq, ik, iw, bias_d)

    x1 = _merge_call(x, shift_a, scale_a, gate_a, o_a, o_b, o_c, lse_c, o_d, w['gate'],
                     w_branch.astype(bf16), w_out.astype(bf16), ln1_g, ln1_b)

    wt = _router_call(x1, shift_f, scale_f, w_router, b_router)
    wd_all = w_exp_down.reshape(N_EXPERTS * D_FF_EXPERT, d).astype(bf16)
    return _moe_call(x1, shift_f, scale_f, gate_f, wt, w_exp_gate.astype(bf16), w_exp_up.astype(bf16),
                     wd_all, ln2_g, ln2_b)


def kernel(x, c, rel_table, w_router, b_router, w_ada, b_ada, w_in, a_lambda_q, a_lambda_k, a_subln_g,
           b_sinks, w_branch, w_out, ln1_g, ln1_b, w_exp_gate, w_exp_up, w_exp_down, ln2_g, ln2_b):
    bsz, seq, d = x.shape
    assert d == D_MODEL and seq % (BAND * C_PATTERNS[-1][1]) == 0 and seq % BAND_TILE == 0
    mod = _ada_call(c, w_ada, b_ada)
    tables = _bias_tables(rel_table, seq)
    for l in range(DEPTH):
        mod_l = tuple(m.reshape(bsz, 1, d) for m in jnp.split(mod[l], 6, axis=-1))
        x = _layer(l, x, mod_l, tables, w_router, b_router, w_in[l], a_lambda_q[l], a_lambda_k[l],
                   a_subln_g[l], b_sinks[l], w_branch[l], w_out[l], ln1_g[l], ln1_b[l],
                   w_exp_gate[l], w_exp_up[l], w_exp_down[l], ln2_g[l], ln2_b[l])
    return x
```

```python
import functools
import math

import numpy as np
import jax
import jax.numpy as jnp
from jax import lax
from jax.experimental import pallas as pl
from jax.experimental.pallas import tpu as pltpu

f32 = jnp.float32
bf16 = jnp.bfloat16
i32 = jnp.int32
i16 = jnp.int16

D_MODEL = 1024
DEPTH = 2
HEAD_DIM = 64
A_HEADS = 4
A_QK = 32
A_V = 2 * A_QK
B_QHEADS = 4
B_KVHEADS = 2
B_WINDOW = 128
C_PATTERNS = ((128, 1), (512, 4), (2048, 16))
C_HEADS = 4
D_HEADS = 4
D_TOPK_MAX = 256
IDX_HEADS = 8
IDX_DIM = 32
REL_BUCKETS = 32
REL_MAX_DIST = 2048
A_BIAS0 = 0
B_BIAS0 = A_BIAS0 + A_HEADS
C_BIAS0 = B_BIAS0 + B_QHEADS
D_BIAS0 = C_BIAS0 + len(C_PATTERNS) * C_HEADS
N_BIAS_HEADS = D_BIAS0 + D_HEADS
N_BRANCH = 4
BRANCH_WIDTH = 256
N_EXPERTS = 16
N_GROUPS = 4
EXPERTS_PER_GROUP = N_EXPERTS // N_GROUPS
D_FF_EXPERT = 256
DEEPNORM_ALPHA = (2 * DEPTH) ** 0.25
LN_EPS = 1e-5

IN_SPLITS = (
    ('a_q', A_HEADS * 2 * A_QK), ('a_k', A_HEADS * 2 * A_QK), ('a_v', A_HEADS * A_V),
    ('b_q', B_QHEADS * HEAD_DIM), ('b_k', B_KVHEADS * HEAD_DIM), ('b_v', B_KVHEADS * HEAD_DIM),
    ('c_q', len(C_PATTERNS) * C_HEADS * HEAD_DIM), ('c_k', len(C_PATTERNS) * C_HEADS * HEAD_DIM),
    ('c_v', len(C_PATTERNS) * C_HEADS * HEAD_DIM),
    ('d_q', D_HEADS * HEAD_DIM), ('d_k', HEAD_DIM), ('d_v', HEAD_DIM),
    ('i_q', IDX_HEADS * IDX_DIM), ('i_k', IDX_DIM), ('i_w', IDX_HEADS),
    ('gate', N_BRANCH * D_MODEL),
)

LANES = 128
SUBLANES = 8
PACK = 16
T = 256
BAND = 128
BAND_TILE = 512
PROJ_TILE = 512
VMEM_LIMIT = 56 * 1024 * 1024
N_FAR = -(-(REL_MAX_DIST + T - 1) // T)

KEY_BLOCKS = 4
LOG2E = math.log2(math.e)

NEG_INIT = -1e30
NEG_MASK = -2e30
INT_MIN = -2 ** 31
I16_MIN = -2 ** 15


def _params(n_axes):
    return pltpu.CompilerParams(dimension_semantics=("arbitrary",) * n_axes,
                                vmem_limit_bytes=VMEM_LIMIT)


def _const_spec(shape):
    nd = len(shape)
    return pl.BlockSpec(shape, lambda *_: (0,) * nd, pipeline_mode=pl.Buffered(1))


def _modulate(x_ref, shift_ref, scale_ref):
    return x_ref[0] * (1.0 + scale_ref[0]) + shift_ref[0]


def _layer_norm(z, g, b):
    mu = jnp.mean(z, axis=-1, keepdims=True)
    zc = z - mu
    var = jnp.mean(zc * zc, axis=-1, keepdims=True)
    return zc * lax.rsqrt(var + LN_EPS) * g + b


def _ada_kernel(c_ref, w_ref, b_ref, o_ref):
    c = c_ref[...]
    cond = c * jax.nn.sigmoid(c)
    o_ref[0] = jnp.dot(cond, w_ref[0], preferred_element_type=f32,
                       precision=lax.Precision.HIGHEST) + b_ref[0]


def _ada_call(c, w_ada, b_ada):
    depth, d, n = w_ada.shape
    bsz = c.shape[0]
    rows = -(-bsz // SUBLANES) * SUBLANES
    c_pad = jnp.zeros((rows, d), f32).at[:bsz].set(c)
    tn = 1536
    out = pl.pallas_call(
        _ada_kernel, name="ada_mod",
        grid=(depth, n // tn),
        in_specs=[pl.BlockSpec((rows, d), lambda l, j: (0, 0)),
                  pl.BlockSpec((1, d, tn), lambda l, j: (l, 0, j)),
                  pl.BlockSpec((1, 1, tn), lambda l, j: (l, 0, j))],
        out_specs=pl.BlockSpec((1, rows, tn), lambda l, j: (l, 0, j)),
        out_shape=jax.ShapeDtypeStruct((depth, rows, n), f32),
        compiler_params=_params(2),
    )(c_pad, w_ada, b_ada.reshape(depth, 1, n))
    return out[:, :bsz]


def _proj_kernel(*refs, kinds):
    n = len(kinds)
    x_ref, shift_ref, scale_ref = refs[:3]
    w_refs = refs[3:3 + n]
    o_refs = refs[3 + n:]
    u = _modulate(x_ref, shift_ref, scale_ref).astype(bf16)
    for kind, w_ref, o_ref in zip(kinds, w_refs, o_refs):
        if kind[0] == 'feat':
            r = lax.dot_general(w_ref[...], u, (((1,), (1,)), ((), ())), preferred_element_type=f32)
            if len(kind) > 1:
                r = r * kind[1]
            for c in range(o_ref.shape[1]):
                o_ref[0, c] = r[:, c * T:(c + 1) * T].astype(o_ref.dtype)
        elif kind[0] == 'tok':
            r = jnp.dot(u, w_ref[...], preferred_element_type=f32)
            o_ref[0] = r.astype(o_ref.dtype)
        else:
            _, heads, hd = kind
            r = jnp.dot(u, w_ref[...], preferred_element_type=f32)
            for h in range(heads):
                o_ref[0, h] = r[:, h * hd:(h + 1) * hd].astype(o_ref.dtype)


def _proj_call(name, x, shift, scale, w_cols, kinds, dtypes):
    bsz, seq, d = x.shape
    pt = PROJ_TILE
    weights, w_specs, out_specs, out_shapes = [], [], [], []
    for w, kind, dt in zip(w_cols, kinds, dtypes):
        feat = w.shape[1]
        if kind[0] == 'feat':
            weights.append(w.T)
            w_specs.append(_const_spec((feat, d)))
            out_specs.append(pl.BlockSpec((1, pt // T, feat, T), lambda b, i: (b, i, 0, 0)))
            out_shapes.append(jax.ShapeDtypeStruct((bsz, seq // T, feat, T), dt))
        elif kind[0] == 'tok':
            weights.append(w)
            w_specs.append(_const_spec((d, feat)))
            out_specs.append(pl.BlockSpec((1, pt, feat), lambda b, i: (b, i, 0)))
            out_shapes.append(jax.ShapeDtypeStruct((bsz, seq, feat), dt))
        else:
            _, heads, hd = kind
            weights.append(w)
            w_specs.append(_const_spec((d, feat)))
            out_specs.append(pl.BlockSpec((1, heads, pt, hd), lambda b, i: (b, 0, i, 0)))
            out_shapes.append(jax.ShapeDtypeStruct((bsz, heads, seq, hd), dt))
    mod_spec = pl.BlockSpec((1, 1, d), lambda b, i: (b, 0, 0))
    return pl.pallas_call(
        functools.partial(_proj_kernel, kinds=tuple(kinds)), name=name,
        grid=(bsz, seq // pt),
        in_specs=[pl.BlockSpec((1, pt, d), lambda b, i: (b, i, 0)), mod_spec, mod_spec] + w_specs,
        out_specs=out_specs, out_shape=out_shapes,
        compiler_params=_params(2),
    )(x, shift, scale, *weights)


def _softmax_step(s, v_t, state):
    mx, acc = state
    mx_new = jnp.maximum(mx, jnp.max(s, axis=0, keepdims=True))
    alpha = jnp.exp2(mx - mx_new)
    p = jnp.exp2(s - mx_new).astype(bf16)
    v_ext = jnp.concatenate([v_t, jnp.ones((PACK, v_t.shape[1]), bf16)], axis=0)
    acc = alpha * acc + jnp.dot(v_ext, p, preferred_element_type=f32)
    return mx_new, acc


def _softmax_init(dv):
    return (jnp.full((1, T), NEG_INIT, f32), jnp.zeros((dv + PACK, T), f32))


def _softmax_result(state, dv):
    _, acc = state
    return acc[:dv] / acc[dv:dv + 1]


def _causal_tile():
    key = lax.broadcasted_iota(i32, (T, T), 0)
    qry = lax.broadcasted_iota(i32, (T, T), 1)
    return key <= qry


def _sweep_key_tiles(i, tile_logits, tail_mask, tile_values, state, buf_a, buf_b):
    n_heads = len(state)
    n_full = i // KEY_BLOCKS

    def tile_of(g, t):
        j = g * KEY_BLOCKS + t
        return j, jnp.minimum(j, i)

    def fill(buf, g):
        for t in range(KEY_BLOCKS):
            for h, s in enumerate(tile_logits(*tile_of(g, t))):
                buf[t * n_heads + h] = s

    def drain(buf, g, st, tail):
        for t in range(KEY_BLOCKS):
            j, jc = tile_of(g, t)
            v_t = tile_values(jc)
            logits = [buf[t * n_heads + h] for h in range(n_heads)]
            if tail and tail_mask is not None:
                logits = [tail_mask(j, s) for s in logits]
            st = tuple(_softmax_step(s, v_t, st_h) for s, st_h in zip(logits, st))
        return st

    def pair(gp, st):
        g = 2 * gp
        fill(buf_b, g + 1)
        st = drain(buf_a, g, st, False)
        fill(buf_a, g + 2)
        return drain(buf_b, g + 1, st, False)

    fill(buf_a, 0)
    state = lax.fori_loop(0, n_full // 2, pair, state)
    g_last = n_full - n_full % 2

    def one_more(st):
        fill(buf_b, g_last + 1)
        st = drain(buf_a, g_last, st, False)
        return drain(buf_b, g_last + 1, st, True)

    return lax.cond(n_full % 2 == 1, one_more, lambda st: drain(buf_a, g_last, st, True), state)


def _logit_buffers(n_heads):
    return [pltpu.VMEM((KEY_BLOCKS * n_heads, T, T), f32)] * 2


def _diff_attn_kernel(lq_ref, lk_ref, q_ref, k_ref, v_ref, bias_ref, g_ref, o_ref, buf_a, buf_b,
                      *, lam_init):
    i = pl.program_id(2)
    q_t = q_ref[0, 0]
    q_parts = (q_t[:A_QK], q_t[A_QK:])
    key_minus_query = lax.broadcasted_iota(i32, (T, T), 0) - lax.broadcasted_iota(i32, (T, T), 1)

    def tile_logits(j, jc):
        start = pl.multiple_of(jc * T, T)
        bias = bias_ref[0, jnp.minimum(i - jc, N_FAR)]
        return [jnp.dot(k_ref[0, m, pl.ds(start, T), :], q_parts[m], preferred_element_type=f32) + bias
                for m in range(2)]

    def tail_mask(j, s):
        return jnp.where(key_minus_query <= (i - j) * T, s, NEG_MASK)

    state = _sweep_key_tiles(i, tile_logits, tail_mask, lambda j: v_ref[0, j],
                             (_softmax_init(A_V), _softmax_init(A_V)), buf_a, buf_b)
    lq = lq_ref[...]
    lk = lk_ref[...]
    lam = (jnp.exp(jnp.sum(lq[0:1] * lk[0:1], axis=1, keepdims=True))
           - jnp.exp(jnp.sum(lq[1:2] * lk[1:2], axis=1, keepdims=True)) + lam_init)
    o = _softmax_result(state[0], A_V) - lam * _softmax_result(state[1], A_V)
    ms = jnp.mean(o * o, axis=0, keepdims=True)
    o = o * lax.rsqrt(ms + LN_EPS) * g_ref[...] * (1.0 - lam_init)
    o_ref[0, 0] = o.astype(o_ref.dtype)


def _diff_attn_call(q_t, k_h, v_t, bias, lam_q, lam_k, subln_g, layer_idx):
    bsz, nt, _, _ = q_t.shape
    seq = nt * T
    lam_init = 0.8 - 0.6 * math.exp(-0.3 * layer_idx)
    g = jnp.broadcast_to(subln_g.astype(f32)[:, None], (A_V, T))
    nb = bias.shape[1]
    return pl.pallas_call(
        functools.partial(_diff_attn_kernel, lam_init=lam_init), name="mixer_a_diff_attn",
        grid=(bsz, A_HEADS, nt),
        in_specs=[_const_spec((2, A_QK)), _const_spec((2, A_QK)),
                  pl.BlockSpec((1, 1, 2 * A_QK, T), lambda b, h, i: (b, i, h, 0)),
                  pl.BlockSpec((1, 2, seq, A_QK), lambda b, h, i: (b, h, 0, 0)),
                  pl.BlockSpec((1, nt, A_V, T), lambda b, h, i: (b, 0, h, 0)),
                  pl.BlockSpec((1, nb, T, T), lambda b, h, i: (h, 0, 0, 0)),
                  _const_spec((A_V, T))],
        out_specs=pl.BlockSpec((1, 1, A_V, T), lambda b, h, i: (b, i, h, 0)),
        out_shape=jax.ShapeDtypeStruct((bsz, nt, A_HEADS * A_V, T), bf16),
        scratch_shapes=_logit_buffers(2),
        compiler_params=_params(3),
    )(lam_q.astype(f32), lam_k.astype(f32), q_t, k_h, v_t, bias, g)


def _sparse_attn_kernel(q_ref, k_ref, v_ref, iq_ref, ik_ref, iw_ref, bias_ref, o_ref,
                        key_ref, hi_ref, lo_ref, buf_a, buf_b, *, k_sel, seq):
    i = pl.program_id(1)
    causal = _causal_tile()
    key_iota = lax.broadcasted_iota(i32, (T, T), 0)
    iq_t = iq_ref[0, 0]
    iw_t = iw_ref[0, 0]

    def score_block(j, diagonal):
        start = pl.multiple_of(j * T, T)
        ikb = ik_ref[0, pl.ds(start, T), :]
        sc = jnp.zeros((T, T), f32)
        for h in range(IDX_HEADS):
            r = jnp.dot(ikb, iq_t[h * IDX_DIM:(h + 1) * IDX_DIM], preferred_element_type=f32)
            sc = sc + jnp.maximum(r, 0.0) * iw_t[h:h + 1]
        bits = pltpu.bitcast(sc, i32)
        key = bits ^ ((bits >> 31) & jnp.int32(0x7FFFFFFF))
        if diagonal:
            key = jnp.where(causal, key, jnp.int32(INT_MIN))
        key_ref[pl.ds(start, T), :] = key
        hi_ref[pl.ds(start, T), :] = (key >> 16).astype(i16)
        lo_ref[pl.ds(start, T), :] = (((key ^ jnp.int32(0x8000)) << 16) >> 16).astype(i16)

    def _score_pair(jj, carry):
        score_block(2 * jj, False)
        score_block(2 * jj + 1, False)
        return carry

    lax.fori_loop(0, i // 2, _score_pair, 0)

    @pl.when(i % 2 == 1)
    def _():
        score_block(i - 1, False)

    score_block(i, True)
    after = pl.multiple_of((i + 1) * T, T)
    hi_ref[pl.ds(after, T), :] = jnp.full((T, T), I16_MIN, i16)
    lo_ref[pl.ds(after, T), :] = jnp.full((T, T), I16_MIN, i16)
    n_chunks = (i + 2) // 2

    def count16(ref, cand):
        def body(c, acc):
            start = pl.multiple_of(c * 2 * T, 2 * T)
            ind = jnp.where(ref[pl.ds(start, 2 * T), :] >= cand, jnp.int16(1), jnp.int16(0))
            for r in range(2 * T // PACK):
                acc = acc + ind[r * PACK:(r + 1) * PACK]
            return acc
        acc = lax.fori_loop(0, n_chunks, body, jnp.zeros((PACK, T), i16))
        return acc.astype(i32).sum(axis=0, keepdims=True)

    def search16(ref, n_start):
        def bit_step(b, carry):
            t_u, n_ge = carry
            cand_u = t_u | lax.shift_left(jnp.int32(1), 15 - b)
            cnt = count16(ref, (cand_u + I16_MIN).astype(i16))
            take = cnt >= k_sel
            return jnp.where(take, cand_u, t_u), jnp.where(take, cnt, n_ge)

        return lax.fori_loop(0, 16, bit_step, (jnp.zeros((1, T), i32), n_start))

    def any_above_k(n):
        return jnp.max(n.astype(f32)) > k_sel

    n_valid = i * T + lax.broadcasted_iota(i32, (1, T), 1) + 1
    hi_u, n_hi = search16(hi_ref, n_valid)

    def refine():
        p16 = (hi_u + I16_MIN).astype(i16)

        def body(c, carry):
            rows = pl.ds(pl.multiple_of(c * 2 * T, 2 * T), 2 * T)
            hi = hi_ref[rows, :]
            lo_ref[rows, :] = jnp.where(hi > p16, jnp.int16(-I16_MIN - 1),
                                        jnp.where(hi == p16, lo_ref[rows, :], jnp.int16(I16_MIN)))
            return carry

        lax.fori_loop(0, n_chunks, body, 0)
        return search16(lo_ref, n_hi)

    lo_u, n_ge = lax.cond(any_above_k(n_hi), refine, lambda: (jnp.zeros((1, T), i32), n_hi))
    thr = jnp.maximum(((hi_u + I16_MIN) << 16) + lo_u, jnp.int32(INT_MIN + 1))

    tied = n_ge > k_sel

    def count(pred):
        def body(j, acc):
            start = pl.multiple_of(j * T, T)
            ind = jnp.where(pred(key_ref[pl.ds(start, T), :], start), 1, 0).astype(i32)
            return acc + ind.reshape(T // SUBLANES, SUBLANES, T).sum(axis=0)
        acc = lax.fori_loop(0, i + 1, body, jnp.zeros((SUBLANES, T), i32))
        return acc.sum(axis=0, keepdims=True)

    def tie_limit():
        need = k_sel - count(lambda kb, _: kb > thr)
        n_bits = max(1, (seq - 1).bit_length())

        def idx_step(b, lim):
            bit = lax.shift_left(jnp.int32(1), n_bits - 1 - b)
            probe = lim + bit - 1
            cnt = count(lambda kb, start: jnp.logical_and(kb == thr, key_iota + start <= probe))
            return jnp.where(cnt < need, lim + bit, lim)

        lim = lax.fori_loop(0, n_bits, idx_step, jnp.zeros((1, T), i32))
        return jnp.where(tied, lim, jnp.int32(seq))

    q_t = q_ref[0, 0]

    def attend(idx_lim):
        def tile_logits(j, jc):
            start = pl.multiple_of(jc * T, T)
            kb = k_ref[0, pl.ds(start, T), :]
            keys = key_ref[pl.ds(start, T), :]
            thr_j = jnp.where(j > i, jnp.int32(-INT_MIN - 1), thr)
            drop = keys < thr_j
            if idx_lim is not None:
                drop = jnp.logical_or(drop, jnp.logical_and(keys == thr_j, key_iota + start > idx_lim))
            return [jnp.where(drop, NEG_MASK,
                              jnp.dot(kb, q_t[h * HEAD_DIM:(h + 1) * HEAD_DIM], preferred_element_type=f32)
                              + bias_ref[h, jnp.minimum(i - jc, N_FAR)]) for h in range(D_HEADS)]

        return _sweep_key_tiles(i, tile_logits, None, lambda j: v_ref[0, j],
                                tuple(_softmax_init(HEAD_DIM) for _ in range(D_HEADS)), buf_a, buf_b)

    state = lax.cond(any_above_k(n_ge), lambda: attend(tie_limit()), lambda: attend(None))
    for h in range(D_HEADS):
        o_ref[0, 0, h * HEAD_DIM:(h + 1) * HEAD_DIM, :] = (
            _softmax_result(state[h], HEAD_DIM).astype(o_ref.dtype))


def _sparse_attn_call(q_t, k, v_t, iq_t, ik, iw_t, bias):
    bsz, nt, _, _ = q_t.shape
    seq = nt * T
    assert seq // PACK < -I16_MIN
    k_sel = min(D_TOPK_MAX, seq // 4)
    tile = lambda f: pl.BlockSpec((1, 1, f, T), lambda b, i: (b, i, 0, 0))
    whole = lambda shape: pl.BlockSpec((1,) + shape, lambda b, i: (b,) + (0,) * len(shape))
    return pl.pallas_call(
        functools.partial(_sparse_attn_kernel, k_sel=k_sel, seq=seq), name="mixer_d_sparse_attn",
        grid=(bsz, nt),
        in_specs=[tile(D_HEADS * HEAD_DIM), whole((seq, HEAD_DIM)), whole((nt, HEAD_DIM, T)),
                  tile(IDX_HEADS * IDX_DIM), whole((seq, IDX_DIM)), tile(IDX_HEADS),
                  _const_spec(bias.shape)],
        out_specs=tile(D_HEADS * HEAD_DIM),
        out_shape=jax.ShapeDtypeStruct((bsz, nt, D_HEADS * HEAD_DIM, T), bf16),
        scratch_shapes=[pltpu.VMEM((seq, T), i32), pltpu.VMEM((seq + T, T), i16),
                        pltpu.VMEM((seq + T, T), i16)] + _logit_buffers(D_HEADS),
        compiler_params=_params(2),
    )(q_t, k, v_t, iq_t, ik, iw_t, bias)


def _band_kernel(*refs, hq, hk, has_sink, tb):
    if has_sink:
        sink_ref, q_ref, kp_ref, kc_ref, vp_ref, vc_ref, bias_ref, o_ref = refs
    else:
        q_ref, kp_ref, kc_ref, vp_ref, vc_ref, bias_ref, o_ref, lse_ref = refs
    i = pl.program_id(1)
    rep = hq // hk
    from_prev_tile = lax.broadcasted_iota(i32, (BAND, 2 * BAND), 1) < BAND
    contract_last = (((1,), (1,)), ((), ()))
    n_sub = tb // BAND
    vals, logits = [], []
    for a in range(n_sub):
        if a == 0:
            kk = jnp.concatenate([kp_ref[0], kc_ref[0, 0:BAND, :]], axis=0)
            vv = jnp.concatenate([vp_ref[0], vc_ref[0, 0:BAND, :]], axis=0)
        else:
            kk = kc_ref[0, (a - 1) * BAND:(a + 1) * BAND, :]
            vv = vc_ref[0, (a - 1) * BAND:(a + 1) * BAND, :]
        qa = q_ref[0, a * BAND:(a + 1) * BAND, :]
        row = []
        for h in range(hq):
            g = h // rep
            hs = slice(h * HEAD_DIM, (h + 1) * HEAD_DIM)
            gs = slice(g * HEAD_DIM, (g + 1) * HEAD_DIM)
            s = lax.dot_general(qa[:, hs], kk[:, gs], contract_last, preferred_element_type=f32)
            s = s * (HEAD_DIM ** -0.5) + bias_ref[h]
            if a == 0:
                s = jnp.where(jnp.logical_and(from_prev_tile, i == 0), NEG_MASK, s)
            row.append(s)
        logits.append(row)
        vals.append(vv)
    for a in range(n_sub):
        outs, lses = [], []
        for h in range(hq):
            g = h // rep
            gs = slice(g * HEAD_DIM, (g + 1) * HEAD_DIM)
            s = logits[a][h]
            m = jnp.max(s, axis=1, keepdims=True)
            if has_sink:
                m = jnp.maximum(m, sink_ref[h])
            e = jnp.exp(s - m)
            den = jnp.sum(e, axis=1, keepdims=True)
            if has_sink:
                den = den + jnp.exp(sink_ref[h] - m)
            o = jnp.dot(e.astype(bf16), vals[a][:, gs], preferred_element_type=f32) / den
            outs.append(o)
            if not has_sink:
                lses.append(m + jnp.log(den))
        rows = slice(a * BAND, (a + 1) * BAND)
        o_ref[0, rows, :] = jnp.concatenate(outs, axis=1).astype(o_ref.dtype)
        if not has_sink:
            lse_ref[0, rows, :] = jnp.concatenate(lses, axis=1)


def _band_call(name, q, k, v, bias, sinks):
    bn, sn, qw = q.shape
    kw = k.shape[2]
    hq, hk = qw // HEAD_DIM, kw // HEAD_DIM
    tb = min(BAND_TILE, sn)
    per = tb // BAND
    has_sink = sinks is not None
    cur = lambda w: pl.BlockSpec((1, tb, w), lambda b, i: (b, i, 0))
    prev = lambda w: pl.BlockSpec((1, BAND, w), lambda b, i: (b, jnp.maximum(i * per - 1, 0), 0))
    in_specs = [cur(qw), prev(kw), cur(kw), prev(kw), cur(kw), _const_spec(bias.shape)]
    args = [q, k, k, v, v, bias]
    if has_sink:
        in_specs = [pl.BlockSpec(memory_space=pltpu.SMEM)] + in_specs
        args = [sinks.astype(f32)] + args
        out_specs = cur(qw)
        out_shape = jax.ShapeDtypeStruct((bn, sn, qw), bf16)
    else:
        out_specs = [cur(qw), cur(hq)]
        out_shape = [jax.ShapeDtypeStruct((bn, sn, qw), f32), jax.ShapeDtypeStruct((bn, sn, hq), f32)]
    return pl.pallas_call(
        functools.partial(_band_kernel, hq=hq, hk=hk, has_sink=has_sink, tb=tb), name=name,
        grid=(bn, sn // tb), in_specs=in_specs, out_specs=out_specs, out_shape=out_shape,
        compiler_params=_params(2),
    )(*args)


def _merge_kernel(x_ref, shift_ref, scale_ref, gate_ref, oa_ref, ob_ref, oc_refs, lse_refs, od_ref,
                  wg_ref, wb_ref, wo_ref, g_ref, b_ref, o_ref):
    x = x_ref[0]
    u = _modulate(x_ref, shift_ref, scale_ref).astype(bf16)
    lse = [r[0] for r in lse_refs]
    top = jnp.maximum(jnp.maximum(lse[0], lse[1]), lse[2])
    ws = [jnp.exp(v - top) for v in lse]
    ws = [jnp.concatenate([jnp.broadcast_to(w[:, h:h + 1], (T, HEAD_DIM)) for h in range(C_HEADS)], axis=1)
          for w in ws]
    o_c = (ws[0] * oc_refs[0][0] + ws[1] * oc_refs[1][0] + ws[2] * oc_refs[2][0]) / (ws[0] + ws[1] + ws[2])
    contract_first = (((0,), (0,)), ((), ()))
    branch = (
        lax.dot_general(oa_ref[0, 0], wb_ref[0], contract_first, preferred_element_type=f32),
        jnp.dot(ob_ref[0], wb_ref[1], preferred_element_type=f32),
        jnp.dot(o_c.astype(bf16), wb_ref[2], preferred_element_type=f32),
        lax.dot_general(od_ref[0, 0], wb_ref[3], contract_first, preferred_element_type=f32),
    )
    merged = None
    for n in range(N_BRANCH):
        logits = jnp.dot(u, wg_ref[:, n * D_MODEL:(n + 1) * D_MODEL], preferred_element_type=f32)
        term = jax.nn.sigmoid(logits) * branch[n]
        merged = term if merged is None else merged + term
    y = jnp.dot(merged.astype(bf16), wo_ref[...], preferred_element_type=f32)
    z = DEEPNORM_ALPHA * x + (1.0 + gate_ref[0]) * y
    o_ref[0] = _layer_norm(z, g_ref[...], b_ref[...])


def _merge_call(x, shift, scale, gate, o_a, o_b, o_c, lse_c, o_d, w_gate, w_branch, w_out, ln_g, ln_b):
    bsz, seq, d = x.shape
    nt = seq // T
    tok = lambda w: pl.BlockSpec((1, T, w), lambda b, i: (b, i, 0))
    feat = lambda w: pl.BlockSpec((1, 1, w, T), lambda b, i: (b, i, 0, 0))
    mod = pl.BlockSpec((1, 1, d), lambda b, i: (b, 0, 0))

    def kernel(*refs):
        (x_ref, s_ref, c_ref, g_ref, oa, ob, c0, c1, c2, l0, l1, l2, od, wg, wb, wo, lg, lb, o) = refs
        _merge_kernel(x_ref, s_ref, c_ref, g_ref, oa, ob, (c0, c1, c2), (l0, l1, l2), od, wg, wb, wo, lg, lb, o)

    return pl.pallas_call(
        kernel, name="merge_outproj_ln1",
        grid=(bsz, nt),
        in_specs=[tok(d), mod, mod, mod, feat(BRANCH_WIDTH), tok(BRANCH_WIDTH)]
                 + [tok(BRANCH_WIDTH)] * 3 + [tok(C_HEADS)] * 3 + [feat(BRANCH_WIDTH),
                 _const_spec(w_gate.shape), _const_spec(w_branch.shape), _const_spec(w_out.shape),
                 _const_spec((1, d)), _const_spec((1, d))],
        out_specs=tok(d),
        out_shape=jax.ShapeDtypeStruct((bsz, seq, d), f32),
        compiler_params=_params(2),
    )(x, shift, scale, gate, o_a, o_b, *o_c, *lse_c, o_d, w_gate, w_branch, w_out,
      ln_g.reshape(1, d), ln_b.reshape(1, d))


ROUTER_TILE = 512


def _router_kernel(x_ref, shift_ref, scale_ref, wr_ref, br_ref, o_ref):
    v = _modulate(x_ref, shift_ref, scale_ref)
    logits = lax.dot_general(wr_ref[...], v, (((1,), (1,)), ((), ())), preferred_element_type=f32,
                             precision=lax.Precision.HIGHEST)
    e = jnp.exp(logits - jnp.max(logits, axis=0, keepdims=True))
    probs = e / jnp.sum(e, axis=0, keepdims=True)
    score = probs + br_ref[...]
    p = [probs[n:n + 1] for n in range(N_EXPERTS)]
    s = [score[n:n + 1] for n in range(N_EXPERTS)]
    best_g = None
    for g in range(N_GROUPS):
        a, b, c, d = s[g * EXPERTS_PER_GROUP:(g + 1) * EXPERTS_PER_GROUP]
        hi1, lo1, hi2, lo2 = jnp.maximum(a, b), jnp.minimum(a, b), jnp.maximum(c, d), jnp.minimum(c, d)
        gs = jnp.maximum(hi1, hi2) + jnp.maximum(jnp.minimum(hi1, hi2), jnp.maximum(lo1, lo2))
        if best_g is None:
            best_g, grp = gs, jnp.zeros_like(gs, dtype=i32)
        else:
            better = gs > best_g
            best_g = jnp.where(better, gs, best_g)
            grp = jnp.where(better, g, grp)
    neg_inf = jnp.float32(-jnp.inf)
    cand = [jnp.where(grp == n // EXPERTS_PER_GROUP, s[n], neg_inf) for n in range(N_EXPERTS)]

    def first_argmax(vals):
        best, idx = vals[0], jnp.zeros_like(grp)
        for n in range(1, N_EXPERTS):
            better = vals[n] > best
            best = jnp.where(better, vals[n], best)
            idx = jnp.where(better, n, idx)
        return idx

    i1 = first_argmax(cand)
    i2 = first_argmax([jnp.where(i1 == n, neg_inf, cand[n]) for n in range(N_EXPERTS)])
    w1 = sum(jnp.where(i1 == n, p[n], 0.0) for n in range(N_EXPERTS))
    w2 = sum(jnp.where(i2 == n, p[n], 0.0) for n in range(N_EXPERTS))
    tot = w1 + w2
    rows = [jnp.where(i1 == n, w1 / tot, jnp.where(i2 == n, w2 / tot, 0.0)) for n in range(N_EXPERTS)]
    o_ref[0] = jnp.concatenate(rows, axis=0)


def _router_call(x, shift, scale, w_router, b_router):
    bsz, seq, d = x.shape
    rt = min(ROUTER_TILE, seq)
    mod = pl.BlockSpec((1, 1, d), lambda b, i: (b, 0, 0))
    return pl.pallas_call(
        _router_kernel, name="moe_router",
        grid=(bsz, seq // rt),
        in_specs=[pl.BlockSpec((1, rt, d), lambda b, i: (b, i, 0)), mod, mod,
                  _const_spec((N_EXPERTS, d)), _const_spec((N_EXPERTS, 1))],
        out_specs=pl.BlockSpec((1, N_EXPERTS, rt), lambda b, i: (b, 0, i)),
        out_shape=jax.ShapeDtypeStruct((bsz, N_EXPERTS, seq), f32),
        compiler_params=_params(2),
    )(x, shift, scale, w_router.T.astype(f32), b_router.astype(f32).reshape(N_EXPERTS, 1))


GROUP_CAP = 96
MOE_TILE = 512


def _expert_mlp(v, w_cols, experts, wg_ref, wu_ref, wd_ref):
    parts = []
    for w_n, n in zip(w_cols, experts):
        hg = jnp.dot(v, wg_ref[n], preferred_element_type=f32)
        hu = jnp.dot(v, wu_ref[n], preferred_element_type=f32)
        act = hg * jax.nn.sigmoid(hg) * hu
        parts.append(jnp.where(w_n != 0.0, act * w_n, 0.0).astype(bf16))
    h = jnp.concatenate(parts, axis=1)
    rows = pl.ds(experts[0] * D_FF_EXPERT, len(experts) * D_FF_EXPERT)
    return jnp.dot(h, wd_ref[rows, :], preferred_element_type=f32)


def _moe_kernel(x_ref, shift_ref, scale_ref, gate_ref, wt_ref, wtt_ref, wg_ref, wu_ref, wd_ref,
                g_ref, b_ref, o_ref):
    n_sub = x_ref.shape[1] // T
    x = x_ref[0]
    v = _modulate(x_ref, shift_ref, scale_ref).astype(bf16)
    wt = wt_ref[0]
    wt_t = wtt_ref[0]

    member_f = jnp.concatenate(
        [jnp.where(jnp.sum(wt_t[g * EXPERTS_PER_GROUP:(g + 1) * EXPERTS_PER_GROUP], axis=0, keepdims=True)
                   > 0.0, 1.0, 0.0) for g in range(N_GROUPS)]
        + [jnp.zeros((SUBLANES - N_GROUPS, n_sub * T), f32)], axis=0)
    earlier = (lax.broadcasted_iota(i32, (T, T), 0) < lax.broadcasted_iota(i32, (T, T), 1))
    earlier_b = jnp.where(earlier, 1.0, 0.0).astype(bf16)
    subs = [slice(n * T, (n + 1) * T) for n in range(n_sub)]
    rank = [jnp.dot(member_f[:, sub].astype(bf16), earlier_b, preferred_element_type=f32)
            for sub in subs]
    count = functools.reduce(jnp.maximum, [jnp.max(r + member_f[:, sub]) for r, sub in zip(rank, subs)])

    def grouped():
        slot = lax.broadcasted_iota(i32, (GROUP_CAP, T), 0).astype(f32)
        back = (((0,), (0,)), ((), ()))
        y = [jnp.zeros((T, D_MODEL), f32) for _ in subs]
        for g in range(N_GROUPS):
            experts = list(range(g * EXPERTS_PER_GROUP, (g + 1) * EXPERTS_PER_GROUP))
            picks, rows, weights = [], [], []
            for r, sub in zip(rank, subs):
                pick = jnp.logical_and(slot == r[g:g + 1], member_f[g:g + 1, sub] > 0.5)
                pick_f = jnp.where(pick, 1.0, 0.0)
                picks.append(pick_f.astype(bf16))
                rows.append(jnp.dot(picks[-1], v[sub], preferred_element_type=f32).astype(bf16))
                weights.append(jnp.dot(pick_f, wt[sub, experts[0]:experts[-1] + 1],
                                       preferred_element_type=f32, precision=lax.Precision.HIGHEST))
            v_g = jnp.concatenate(rows, axis=0)
            w_g = jnp.concatenate(weights, axis=0)
            y_g = _expert_mlp(v_g, [w_g[:, k:k + 1] for k in range(EXPERTS_PER_GROUP)], experts,
                              wg_ref, wu_ref, wd_ref)
            y_hi = y_g.astype(bf16)
            y_lo = (y_g - y_hi.astype(f32)).astype(bf16)
            for n, pick_b in enumerate(picks):
                part = slice(n * GROUP_CAP, (n + 1) * GROUP_CAP)
                y[n] = (y[n] + lax.dot_general(pick_b, y_hi[part], back, preferred_element_type=f32)
                        + lax.dot_general(pick_b, y_lo[part], back, preferred_element_type=f32))
        return jnp.concatenate(y, axis=0)

    def dense():
        experts = list(range(N_EXPERTS))
        return _expert_mlp(v, [wt[:, n:n + 1] for n in experts], experts, wg_ref, wu_ref, wd_ref)

    y = lax.cond(count <= GROUP_CAP, grouped, dense)
    z = DEEPNORM_ALPHA * x + (1.0 + gate_ref[0]) * y
    o_ref[0] = _layer_norm(z, g_ref[...], b_ref[...])


def _moe_call(x, shift, scale, gate, wt_t, w_gate, w_up, w_down, ln_g, ln_b):
    bsz, seq, d = x.shape
    mt = MOE_TILE
    tok = lambda w: pl.BlockSpec((1, mt, w), lambda b, i: (b, i, 0))
    mod = pl.BlockSpec((1, 1, d), lambda b, i: (b, 0, 0))
    return pl.pallas_call(
        _moe_kernel, name="moe_experts_ln2",
        grid=(bsz, seq // mt),
        in_specs=[tok(d), mod, mod, mod, tok(N_EXPERTS),
                  pl.BlockSpec((1, N_EXPERTS, mt), lambda b, i: (b, 0, i)),
                  _const_spec(w_gate.shape), _const_spec(w_up.shape), _const_spec(w_down.shape),
                  _const_spec((1, d)), _const_spec((1, d))],
        out_specs=tok(d),
        out_shape=jax.ShapeDtypeStruct((bsz, seq, d), f32),
        compiler_params=_params(2),
    )(x, shift, scale, gate, jnp.swapaxes(wt_t, 1, 2), wt_t, w_gate, w_up, w_down,
      ln_g.reshape(1, d), ln_b.reshape(1, d))


def _t5_bucket(n):
    max_exact = REL_BUCKETS // 2
    nf = jnp.maximum(n, 1).astype(f32)
    log_ratio = jnp.log(nf / max_exact) / math.log(REL_MAX_DIST / max_exact)
    large = max_exact + (log_ratio * (REL_BUCKETS - max_exact)).astype(i32)
    large = jnp.minimum(large, REL_BUCKETS - 1)
    return jnp.where(n < max_exact, n, large)


def _toeplitz_kernel(v_ref, o_ref):
    rows, width = o_ref.shape[1], o_ref.shape[2]
    x = jnp.broadcast_to(v_ref[0], (rows, width))
    o_ref[0] = pltpu.roll(x, 0, 1, stride=1, stride_axis=0)


def _toeplitz_call(vec, rows):
    heads, width = vec.shape
    return pl.pallas_call(
        _toeplitz_kernel, name="bias_toeplitz",
        grid=(heads,),
        in_specs=[pl.BlockSpec((1, 1, width), lambda h: (h, 0, 0))],
        out_specs=pl.BlockSpec((1, rows, width), lambda h: (h, 0, 0)),
        out_shape=jax.ShapeDtypeStruct((heads, rows, width), f32),
        compiler_params=_params(1),
    )(vec.reshape(heads, 1, width))


def _bias_tables(rel_table, seq):
    n_cols = (N_FAR + 1) * T
    n_dist = max(seq, n_cols)
    by_dist = rel_table.astype(f32)[_t5_bucket(jnp.arange(n_dist, dtype=i32))].T
    dense_heads = np.r_[A_BIAS0:A_BIAS0 + A_HEADS, D_BIAS0:D_BIAS0 + D_HEADS]
    dense = _toeplitz_call(by_dist[dense_heads, :n_cols] * LOG2E, T)
    dense = dense.reshape(len(dense_heads), T, N_FAR + 1, T).transpose(0, 2, 1, 3)
    bias_a = dense[:A_HEADS]
    bias_d = dense[A_HEADS:]
    width = 4 * BAND

    def band_vec(h0, nh, max_dist, dilation):
        vals = by_dist[h0:h0 + nh, 0:(BAND + 1) * dilation:dilation]
        vals = jnp.where(np.arange(BAND + 1) <= max_dist, vals, NEG_MASK)[:, ::-1]
        return jnp.pad(vals, ((0, 0), (BAND - 1, width - 2 * BAND)), constant_values=NEG_MASK)

    vecs = [band_vec(B_BIAS0, B_QHEADS, B_WINDOW - 1, 1)]
    vecs += [band_vec(C_BIAS0 + g * C_HEADS, C_HEADS, window // dil, dil)
             for g, (window, dil) in enumerate(C_PATTERNS)]
    band = _toeplitz_call(jnp.concatenate(vecs, axis=0), BAND)[:, :, BAND - 1:3 * BAND - 1]
    bias_b = band[:B_QHEADS]
    bias_c = [band[B_QHEADS + g * C_HEADS:B_QHEADS + (g + 1) * C_HEADS] for g in range(len(C_PATTERNS))]
    return bias_a, bias_b, bias_c, bias_d


def _split_cols(w):
    parts, off = {}, 0
    for name, width in IN_SPLITS:
        parts[name] = w[:, off:off + width]
        off += width
    return parts


def _dilate(a, r):
    if r == 1:
        return a
    b, s, f = a.shape
    return a.reshape(b, s // r, r, f).transpose(0, 2, 1, 3).reshape(b * r, s // r, f)


def _undilate(a, r, bsz):
    if r == 1:
        return a
    _, sn, f = a.shape
    return a.reshape(bsz, r, sn, f).transpose(0, 2, 1, 3).reshape(bsz, sn * r, f)


def _layer(l, x, mod, tables, w_router, b_router, w_in, lam_q, lam_k, subln_g, sinks, w_branch, w_out,
           ln1_g, ln1_b, w_exp_gate, w_exp_up, w_exp_down, ln2_g, ln2_b):
    bsz, seq, d = x.shape
    shift_a, scale_a, gate_a, shift_f, scale_f, gate_f = mod
    bias_a, bias_b, bias_c, bias_d = tables
    w = _split_cols(w_in.astype(bf16))
    hw = C_HEADS * HEAD_DIM

    aq, ak, av = _proj_call("proj_a", x, shift_a, scale_a, [w['a_q'], w['a_k'], w['a_v']],
                            [('feat', A_QK ** -0.5 * LOG2E), ('tokh', 2 * A_HEADS, A_QK), ('feat',)],
                            [bf16] * 3)
    o_a = _diff_attn_call(aq, ak, av, bias_a, lam_q, lam_k, subln_g, l)

    bq, bk, bv = _proj_call("proj_b", x, shift_a, scale_a, [w['b_q'], w['b_k'], w['b_v']],
                            [('tok',)] * 3, [bf16] * 3)
    o_b = _band_call("mixer_b_window_attn", bq, bk, bv, bias_b, sinks)

    c_cols = [w[n][:, g * hw:(g + 1) * hw] for g in range(len(C_PATTERNS)) for n in ('c_q', 'c_k', 'c_v')]
    c_proj = _proj_call("proj_c", x, shift_a, scale_a, c_cols, [('tok',)] * 9, [bf16] * 9)
    o_c, lse_c = [], []
    for g, (_, dil) in enumerate(C_PATTERNS):
        qg, kg, vg = (_dilate(a, dil) for a in c_proj[3 * g:3 * g + 3])
        og, lg = _band_call(f"mixer_c_dilated_attn_{g}", qg, kg, vg, bias_c[g], None)
        o_c.append(_undilate(og, dil, bsz))
        lse_c.append(_undilate(lg, dil, bsz))

    dq, dk, dv, iq, ik, iw = _proj_call(
        "proj_d", x, shift_a, scale_a, [w['d_q'], w['d_k'], w['d_v'], w['i_q'], w['i_k'], w['i_w']],
        [('feat', HEAD_DIM ** -0.5 * LOG2E), ('tok',), ('feat',), ('feat',), ('tok',), ('feat',)],
        [bf16] * 5 + [f32])
    o_d = _sparse_attn_call(dq, dk, dv, iq, ik, iw, bias_d)

    x1 = _merge_call(x, shift_a, scale_a, gate_a, o_a, o_b, o_c, lse_c, o_d, w['gate'],
                     w_branch.astype(bf16), w_out.astype(bf16), ln1_g, ln1_b)

    wt = _router_call(x1, shift_f, scale_f, w_router, b_router)
    wd_all = w_exp_down.reshape(N_EXPERTS * D_FF_EXPERT, d).astype(bf16)
    return _moe_call(x1, shift_f, scale_f, gate_f, wt, w_exp_gate.astype(bf16), w_exp_up.astype(bf16),
                     wd_all, ln2_g, ln2_b)


def kernel(x, c, rel_table, w_router, b_router, w_ada, b_ada, w_in, a_lambda_q, a_lambda_k, a_subln_g,
           b_sinks, w_branch, w_out, ln1_g, ln1_b, w_exp_gate, w_exp_up, w_exp_down, ln2_g, ln2_b):
    bsz, seq, d = x.shape
    assert d == D_MODEL and seq % (BAND * C_PATTERNS[-1][1]) == 0 and seq % BAND_TILE == 0
    mod = _ada_call(c, w_ada, b_ada)
    tables = _bias_tables(rel_table, seq)
    for l in range(DEPTH):
        mod_l = tuple(m.reshape(bsz, 1, d) for m in jnp.split(mod[l], 6, axis=-1))
        x = _layer(l, x, mod_l, tables, w_router, b_router, w_in[l], a_lambda_q[l], a_lambda_k[l],
                   a_subln_g[l], b_sinks[l], w_branch[l], w_out[l], ln1_g[l], ln1_b[l],
                   w_exp_gate[l], w_exp_up[l], w_exp_down[l], ln2_g[l], ln2_b[l])
    return x
```

```python
import functools
import math

import numpy as np
import jax
import jax.numpy as jnp
from jax import lax
from jax.experimental import pallas as pl
from jax.experimental.pallas import tpu as pltpu

f32 = jnp.float32
bf16 = jnp.bfloat16
i32 = jnp.int32
i16 = jnp.int16

D_MODEL = 1024
DEPTH = 2
HEAD_DIM = 64
A_HEADS = 4
A_QK = 32
A_V = 2 * A_QK
B_QHEADS = 4
B_KVHEADS = 2
B_WINDOW = 128
C_PATTERNS = ((128, 1), (512, 4), (2048, 16))
C_HEADS = 4
D_HEADS = 4
D_TOPK_MAX = 256
IDX_HEADS = 8
IDX_DIM = 32
REL_BUCKETS = 32
REL_MAX_DIST = 2048
A_BIAS0 = 0
B_BIAS0 = A_BIAS0 + A_HEADS
C_BIAS0 = B_BIAS0 + B_QHEADS
D_BIAS0 = C_BIAS0 + len(C_PATTERNS) * C_HEADS
N_BIAS_HEADS = D_BIAS0 + D_HEADS
N_BRANCH = 4
BRANCH_WIDTH = 256
N_EXPERTS = 16
N_GROUPS = 4
EXPERTS_PER_GROUP = N_EXPERTS // N_GROUPS
D_FF_EXPERT = 256
DEEPNORM_ALPHA = (2 * DEPTH) ** 0.25
LN_EPS = 1e-5

IN_SPLITS = (
    ('a_q', A_HEADS * 2 * A_QK), ('a_k', A_HEADS * 2 * A_QK), ('a_v', A_HEADS * A_V),
    ('b_q', B_QHEADS * HEAD_DIM), ('b_k', B_KVHEADS * HEAD_DIM), ('b_v', B_KVHEADS * HEAD_DIM),
    ('c_q', len(C_PATTERNS) * C_HEADS * HEAD_DIM), ('c_k', len(C_PATTERNS) * C_HEADS * HEAD_DIM),
    ('c_v', len(C_PATTERNS) * C_HEADS * HEAD_DIM),
    ('d_q', D_HEADS * HEAD_DIM), ('d_k', HEAD_DIM), ('d_v', HEAD_DIM),
    ('i_q', IDX_HEADS * IDX_DIM), ('i_k', IDX_DIM), ('i_w', IDX_HEADS),
    ('gate', N_BRANCH * D_MODEL),
)

LANES = 128
SUBLANES = 8
PACK = 16
T = 256
BAND = 128
BAND_TILE = 512
PROJ_TILE = 512
VMEM_LIMIT = 56 * 1024 * 1024
N_FAR = -(-(REL_MAX_DIST + T - 1) // T)

KEY_BLOCKS = 2
LOG2E = math.log2(math.e)

NEG_INIT = -1e30
NEG_MASK = -2e30
INT_MIN = -2 ** 31
I16_MIN = -2 ** 15


def _params(n_axes):
    return pltpu.CompilerParams(dimension_semantics=("arbitrary",) * n_axes,
                                vmem_limit_bytes=VMEM_LIMIT)


def _const_spec(shape):
    nd = len(shape)
    return pl.BlockSpec(shape, lambda *_: (0,) * nd, pipeline_mode=pl.Buffered(1))


def _modulate(x_ref, shift_ref, scale_ref):
    return x_ref[0] * (1.0 + scale_ref[0]) + shift_ref[0]


def _layer_norm(z, g, b):
    mu = jnp.mean(z, axis=-1, keepdims=True)
    zc = z - mu
    var = jnp.mean(zc * zc, axis=-1, keepdims=True)
    return zc * lax.rsqrt(var + LN_EPS) * g + b


def _ada_kernel(c_ref, w_ref, b_ref, o_ref):
    c = c_ref[...]
    cond = c * jax.nn.sigmoid(c)
    o_ref[0] = jnp.dot(cond, w_ref[0], preferred_element_type=f32,
                       precision=lax.Precision.HIGHEST) + b_ref[0]


def _ada_call(c, w_ada, b_ada):
    depth, d, n = w_ada.shape
    bsz = c.shape[0]
    rows = -(-bsz // SUBLANES) * SUBLANES
    c_pad = jnp.zeros((rows, d), f32).at[:bsz].set(c)
    tn = 1536
    out = pl.pallas_call(
        _ada_kernel, name="ada_mod",
        grid=(depth, n // tn),
        in_specs=[pl.BlockSpec((rows, d), lambda l, j: (0, 0)),
                  pl.BlockSpec((1, d, tn), lambda l, j: (l, 0, j)),
                  pl.BlockSpec((1, 1, tn), lambda l, j: (l, 0, j))],
        out_specs=pl.BlockSpec((1, rows, tn), lambda l, j: (l, 0, j)),
        out_shape=jax.ShapeDtypeStruct((depth, rows, n), f32),
        compiler_params=_params(2),
    )(c_pad, w_ada, b_ada.reshape(depth, 1, n))
    return out[:, :bsz]


def _proj_kernel(*refs, kinds):
    n = len(kinds)
    x_ref, shift_ref, scale_ref = refs[:3]
    w_refs = refs[3:3 + n]
    o_refs = refs[3 + n:]
    u = _modulate(x_ref, shift_ref, scale_ref).astype(bf16)
    for kind, w_ref, o_ref in zip(kinds, w_refs, o_refs):
        if kind[0] == 'feat':
            r = lax.dot_general(w_ref[...], u, (((1,), (1,)), ((), ())), preferred_element_type=f32)
            if len(kind) > 1:
                r = r * kind[1]
            for c in range(o_ref.shape[1]):
                o_ref[0, c] = r[:, c * T:(c + 1) * T].astype(o_ref.dtype)
        elif kind[0] == 'tok':
            r = jnp.dot(u, w_ref[...], preferred_element_type=f32)
            o_ref[0] = r.astype(o_ref.dtype)
        else:
            _, heads, hd = kind
            r = jnp.dot(u, w_ref[...], preferred_element_type=f32)
            for h in range(heads):
                o_ref[0, h] = r[:, h * hd:(h + 1) * hd].astype(o_ref.dtype)


def _proj_call(name, x, shift, scale, w_cols, kinds, dtypes):
    bsz, seq, d = x.shape
    pt = PROJ_TILE
    weights, w_specs, out_specs, out_shapes = [], [], [], []
    for w, kind, dt in zip(w_cols, kinds, dtypes):
        feat = w.shape[1]
        if kind[0] == 'feat':
            weights.append(w.T)
            w_specs.append(_const_spec((feat, d)))
            out_specs.append(pl.BlockSpec((1, pt // T, feat, T), lambda b, i: (b, i, 0, 0)))
            out_shapes.append(jax.ShapeDtypeStruct((bsz, seq // T, feat, T), dt))
        elif kind[0] == 'tok':
            weights.append(w)
            w_specs.append(_const_spec((d, feat)))
            out_specs.append(pl.BlockSpec((1, pt, feat), lambda b, i: (b, i, 0)))
            out_shapes.append(jax.ShapeDtypeStruct((bsz, seq, feat), dt))
        else:
            _, heads, hd = kind
            weights.append(w)
            w_specs.append(_const_spec((d, feat)))
            out_specs.append(pl.BlockSpec((1, heads, pt, hd), lambda b, i: (b, 0, i, 0)))
            out_shapes.append(jax.ShapeDtypeStruct((bsz, heads, seq, hd), dt))
    mod_spec = pl.BlockSpec((1, 1, d), lambda b, i: (b, 0, 0))
    return pl.pallas_call(
        functools.partial(_proj_kernel, kinds=tuple(kinds)), name=name,
        grid=(bsz, seq // pt),
        in_specs=[pl.BlockSpec((1, pt, d), lambda b, i: (b, i, 0)), mod_spec, mod_spec] + w_specs,
        out_specs=out_specs, out_shape=out_shapes,
        compiler_params=_params(2),
    )(x, shift, scale, *weights)


def _softmax_step(s, v_t, state):
    mx, acc = state
    mx_new = jnp.maximum(mx, jnp.max(s, axis=0, keepdims=True))
    alpha = jnp.exp2(mx - mx_new)
    p = jnp.exp2(s - mx_new).astype(bf16)
    v_ext = jnp.concatenate([v_t, jnp.ones((PACK, v_t.shape[1]), bf16)], axis=0)
    acc = alpha * acc + jnp.dot(v_ext, p, preferred_element_type=f32)
    return mx_new, acc


def _softmax_init(dv):
    return (jnp.full((1, T), NEG_INIT, f32), jnp.zeros((dv + PACK, T), f32))


def _softmax_result(state, dv):
    _, acc = state
    return acc[:dv] / acc[dv:dv + 1]


def _causal_tile():
    key = lax.broadcasted_iota(i32, (T, T), 0)
    qry = lax.broadcasted_iota(i32, (T, T), 1)
    return key <= qry


def _sweep_key_tiles(i, tile_logits, tail_mask, tile_values, state, buf_a, buf_b):
    n_heads = len(state)
    n_full = i // KEY_BLOCKS

    def tile_of(g, t):
        j = g * KEY_BLOCKS + t
        return j, jnp.minimum(j, i)

    def fill(buf, g):
        for t in range(KEY_BLOCKS):
            for h, s in enumerate(tile_logits(*tile_of(g, t))):
                buf[t * n_heads + h] = s

    def drain(buf, g, st, tail):
        for t in range(KEY_BLOCKS):
            j, jc = tile_of(g, t)
            v_t = tile_values(jc)
            logits = [buf[t * n_heads + h] for h in range(n_heads)]
            if tail and tail_mask is not None:
                logits = [tail_mask(j, s) for s in logits]
            st = tuple(_softmax_step(s, v_t, st_h) for s, st_h in zip(logits, st))
        return st

    def pair(gp, st):
        g = 2 * gp
        fill(buf_b, g + 1)
        st = drain(buf_a, g, st, False)
        fill(buf_a, g + 2)
        return drain(buf_b, g + 1, st, False)

    fill(buf_a, 0)
    state = lax.fori_loop(0, n_full // 2, pair, state)
    g_last = n_full - n_full % 2

    def one_more(st):
        fill(buf_b, g_last + 1)
        st = drain(buf_a, g_last, st, False)
        return drain(buf_b, g_last + 1, st, True)

    return lax.cond(n_full % 2 == 1, one_more, lambda st: drain(buf_a, g_last, st, True), state)


def _logit_buffers(n_heads):
    return [pltpu.VMEM((KEY_BLOCKS * n_heads, T, T), f32)] * 2


def _diff_attn_kernel(lq_ref, lk_ref, q_ref, k_ref, v_ref, bias_ref, g_ref, o_ref, buf_a, buf_b,
                      *, lam_init):
    i = pl.program_id(2)
    q_t = q_ref[0, 0]
    q_parts = (q_t[:A_QK], q_t[A_QK:])
    key_minus_query = lax.broadcasted_iota(i32, (T, T), 0) - lax.broadcasted_iota(i32, (T, T), 1)

    def tile_logits(j, jc):
        start = pl.multiple_of(jc * T, T)
        bias = bias_ref[0, jnp.minimum(i - jc, N_FAR)]
        return [jnp.dot(k_ref[0, m, pl.ds(start, T), :], q_parts[m], preferred_element_type=f32) + bias
                for m in range(2)]

    def tail_mask(j, s):
        return jnp.where(key_minus_query <= (i - j) * T, s, NEG_MASK)

    state = _sweep_key_tiles(i, tile_logits, tail_mask, lambda j: v_ref[0, j],
                             (_softmax_init(A_V), _softmax_init(A_V)), buf_a, buf_b)
    lq = lq_ref[...]
    lk = lk_ref[...]
    lam = (jnp.exp(jnp.sum(lq[0:1] * lk[0:1], axis=1, keepdims=True))
           - jnp.exp(jnp.sum(lq[1:2] * lk[1:2], axis=1, keepdims=True)) + lam_init)
    o = _softmax_result(state[0], A_V) - lam * _softmax_result(state[1], A_V)
    ms = jnp.mean(o * o, axis=0, keepdims=True)
    o = o * lax.rsqrt(ms + LN_EPS) * g_ref[...] * (1.0 - lam_init)
    o_ref[0, 0] = o.astype(o_ref.dtype)


def _diff_attn_call(q_t, k_h, v_t, bias, lam_q, lam_k, subln_g, layer_idx):
    bsz, nt, _, _ = q_t.shape
    seq = nt * T
    lam_init = 0.8 - 0.6 * math.exp(-0.3 * layer_idx)
    g = jnp.broadcast_to(subln_g.astype(f32)[:, None], (A_V, T))
    nb = bias.shape[1]
    return pl.pallas_call(
        functools.partial(_diff_attn_kernel, lam_init=lam_init), name="mixer_a_diff_attn",
        grid=(bsz, A_HEADS, nt),
        in_specs=[_const_spec((2, A_QK)), _const_spec((2, A_QK)),
                  pl.BlockSpec((1, 1, 2 * A_QK, T), lambda b, h, i: (b, i, h, 0)),
                  pl.BlockSpec((1, 2, seq, A_QK), lambda b, h, i: (b, h, 0, 0)),
                  pl.BlockSpec((1, nt, A_V, T), lambda b, h, i: (b, 0, h, 0)),
                  pl.BlockSpec((1, nb, T, T), lambda b, h, i: (h, 0, 0, 0)),
                  _const_spec((A_V, T))],
        out_specs=pl.BlockSpec((1, 1, A_V, T), lambda b, h, i: (b, i, h, 0)),
        out_shape=jax.ShapeDtypeStruct((bsz, nt, A_HEADS * A_V, T), bf16),
        scratch_shapes=_logit_buffers(2),
        compiler_params=_params(3),
    )(lam_q.astype(f32), lam_k.astype(f32), q_t, k_h, v_t, bias, g)


def _sparse_attn_kernel(q_ref, k_ref, v_ref, iq_ref, ik_ref, iw_ref, bias_ref, o_ref,
                        key_ref, hi_ref, lo_ref, buf_a, buf_b, *, k_sel, seq):
    i = pl.program_id(1)
    causal = _causal_tile()
    key_iota = lax.broadcasted_iota(i32, (T, T), 0)
    iq_t = iq_ref[0, 0]
    iw_t = iw_ref[0, 0]

    def score_block(j, diagonal):
        start = pl.multiple_of(j * T, T)
        ikb = ik_ref[0, pl.ds(start, T), :]
        sc = jnp.zeros((T, T), f32)
        for h in range(IDX_HEADS):
            r = jnp.dot(ikb, iq_t[h * IDX_DIM:(h + 1) * IDX_DIM], preferred_element_type=f32)
            sc = sc + jnp.maximum(r, 0.0) * iw_t[h:h + 1]
        bits = pltpu.bitcast(sc, i32)
        key = bits ^ ((bits >> 31) & jnp.int32(0x7FFFFFFF))
        if diagonal:
            key = jnp.where(causal, key, jnp.int32(INT_MIN))
        key_ref[pl.ds(start, T), :] = key
        hi_ref[pl.ds(start, T), :] = (key >> 16).astype(i16)
        lo_ref[pl.ds(start, T), :] = (((key ^ jnp.int32(0x8000)) << 16) >> 16).astype(i16)

    def _score_pair(jj, carry):
        score_block(2 * jj, False)
        score_block(2 * jj + 1, False)
        return carry

    lax.fori_loop(0, i // 2, _score_pair, 0)

    @pl.when(i % 2 == 1)
    def _():
        score_block(i - 1, False)

    score_block(i, True)
    after = pl.multiple_of((i + 1) * T, T)
    hi_ref[pl.ds(after, T), :] = jnp.full((T, T), I16_MIN, i16)
    lo_ref[pl.ds(after, T), :] = jnp.full((T, T), I16_MIN, i16)
    n_chunks = (i + 2) // 2

    def count16(ref, cand):
        def body(c, acc):
            start = pl.multiple_of(c * 2 * T, 2 * T)
            ind = jnp.where(ref[pl.ds(start, 2 * T), :] >= cand, jnp.int16(1), jnp.int16(0))
            for r in range(2 * T // PACK):
                acc = acc + ind[r * PACK:(r + 1) * PACK]
            return acc
        acc = lax.fori_loop(0, n_chunks, body, jnp.zeros((PACK, T), i16))
        return acc.astype(i32).sum(axis=0, keepdims=True)

    def search16(ref, n_start):
        def bit_step(b, carry):
            t_u, n_ge = carry
            cand_u = t_u | lax.shift_left(jnp.int32(1), 15 - b)
            cnt = count16(ref, (cand_u + I16_MIN).astype(i16))
            take = cnt >= k_sel
            return jnp.where(take, cand_u, t_u), jnp.where(take, cnt, n_ge)

        return lax.fori_loop(0, 16, bit_step, (jnp.zeros((1, T), i32), n_start))

    def any_above_k(n):
        return jnp.max(n.astype(f32)) > k_sel

    n_valid = i * T + lax.broadcasted_iota(i32, (1, T), 1) + 1
    hi_u, n_hi = search16(hi_ref, n_valid)

    def refine():
        p16 = (hi_u + I16_MIN).astype(i16)

        def body(c, carry):
            rows = pl.ds(pl.multiple_of(c * 2 * T, 2 * T), 2 * T)
            hi = hi_ref[rows, :]
            lo_ref[rows, :] = jnp.where(hi > p16, jnp.int16(-I16_MIN - 1),
                                        jnp.where(hi == p16, lo_ref[rows, :], jnp.int16(I16_MIN)))
            return carry

        lax.fori_loop(0, n_chunks, body, 0)
        return search16(lo_ref, n_hi)

    lo_u, n_ge = lax.cond(any_above_k(n_hi), refine, lambda: (jnp.zeros((1, T), i32), n_hi))
    thr = jnp.maximum(((hi_u + I16_MIN) << 16) + lo_u, jnp.int32(INT_MIN + 1))

    tied = n_ge > k_sel

    def count(pred):
        def body(j, acc):
            start = pl.multiple_of(j * T, T)
            ind = jnp.where(pred(key_ref[pl.ds(start, T), :], start), 1, 0).astype(i32)
            return acc + ind.reshape(T // SUBLANES, SUBLANES, T).sum(axis=0)
        acc = lax.fori_loop(0, i + 1, body, jnp.zeros((SUBLANES, T), i32))
        return acc.sum(axis=0, keepdims=True)

    def tie_limit():
        need = k_sel - count(lambda kb, _: kb > thr)
        n_bits = max(1, (seq - 1).bit_length())

        def idx_step(b, lim):
            bit = lax.shift_left(jnp.int32(1), n_bits - 1 - b)
            probe = lim + bit - 1
            cnt = count(lambda kb, start: jnp.logical_and(kb == thr, key_iota + start <= probe))
            return jnp.where(cnt < need, lim + bit, lim)

        lim = lax.fori_loop(0, n_bits, idx_step, jnp.zeros((1, T), i32))
        return jnp.where(tied, lim, jnp.int32(seq))

    q_t = q_ref[0, 0]

    def attend(idx_lim):
        def tile_logits(j, jc):
            start = pl.multiple_of(jc * T, T)
            kb = k_ref[0, pl.ds(start, T), :]
            keys = key_ref[pl.ds(start, T), :]
            thr_j = jnp.where(j > i, jnp.int32(-INT_MIN - 1), thr)
            drop = keys < thr_j
            if idx_lim is not None:
                drop = jnp.logical_or(drop, jnp.logical_and(keys == thr_j, key_iota + start > idx_lim))
            return [jnp.where(drop, NEG_MASK,
                              jnp.dot(kb, q_t[h * HEAD_DIM:(h + 1) * HEAD_DIM], preferred_element_type=f32)
                              + bias_ref[h, jnp.minimum(i - jc, N_FAR)]) for h in range(D_HEADS)]

        return _sweep_key_tiles(i, tile_logits, None, lambda j: v_ref[0, j],
                                tuple(_softmax_init(HEAD_DIM) for _ in range(D_HEADS)), buf_a, buf_b)

    state = lax.cond(any_above_k(n_ge), lambda: attend(tie_limit()), lambda: attend(None))
    for h in range(D_HEADS):
        o_ref[0, 0, h * HEAD_DIM:(h + 1) * HEAD_DIM, :] = (
            _softmax_result(state[h], HEAD_DIM).astype(o_ref.dtype))


def _sparse_attn_call(q_t, k, v_t, iq_t, ik, iw_t, bias):
    bsz, nt, _, _ = q_t.shape
    seq = nt * T
    assert seq // PACK < -I16_MIN
    k_sel = min(D_TOPK_MAX, seq // 4)
    tile = lambda f: pl.BlockSpec((1, 1, f, T), lambda b, i: (b, i, 0, 0))
    whole = lambda shape: pl.BlockSpec((1,) + shape, lambda b, i: (b,) + (0,) * len(shape))
    return pl.pallas_call(
        functools.partial(_sparse_attn_kernel, k_sel=k_sel, seq=seq), name="mixer_d_sparse_attn",
        grid=(bsz, nt),
        in_specs=[tile(D_HEADS * HEAD_DIM), whole((seq, HEAD_DIM)), whole((nt, HEAD_DIM, T)),
                  tile(IDX_HEADS * IDX_DIM), whole((seq, IDX_DIM)), tile(IDX_HEADS),
                  _const_spec(bias.shape)],
        out_specs=tile(D_HEADS * HEAD_DIM),
        out_shape=jax.ShapeDtypeStruct((bsz, nt, D_HEADS * HEAD_DIM, T), bf16),
        scratch_shapes=[pltpu.VMEM((seq, T), i32), pltpu.VMEM((seq + T, T), i16),
                        pltpu.VMEM((seq + T, T), i16)] + _logit_buffers(D_HEADS),
        compiler_params=_params(2),
    )(q_t, k, v_t, iq_t, ik, iw_t, bias)


def _band_kernel(*refs, hq, hk, has_sink, tb):
    if has_sink:
        sink_ref, q_ref, kp_ref, kc_ref, vp_ref, vc_ref, bias_ref, o_ref = refs
    else:
        q_ref, kp_ref, kc_ref, vp_ref, vc_ref, bias_ref, o_ref, lse_ref = refs
    i = pl.program_id(1)
    rep = hq // hk
    from_prev_tile = lax.broadcasted_iota(i32, (BAND, 2 * BAND), 1) < BAND
    contract_last = (((1,), (1,)), ((), ()))
    n_sub = tb // BAND
    vals, logits = [], []
    for a in range(n_sub):
        if a == 0:
            kk = jnp.concatenate([kp_ref[0], kc_ref[0, 0:BAND, :]], axis=0)
            vv = jnp.concatenate([vp_ref[0], vc_ref[0, 0:BAND, :]], axis=0)
        else:
            kk = kc_ref[0, (a - 1) * BAND:(a + 1) * BAND, :]
            vv = vc_ref[0, (a - 1) * BAND:(a + 1) * BAND, :]
        qa = q_ref[0, a * BAND:(a + 1) * BAND, :]
        row = []
        for h in range(hq):
            g = h // rep
            hs = slice(h * HEAD_DIM, (h + 1) * HEAD_DIM)
            gs = slice(g * HEAD_DIM, (g + 1) * HEAD_DIM)
            s = lax.dot_general(qa[:, hs], kk[:, gs], contract_last, preferred_element_type=f32)
            s = s * (HEAD_DIM ** -0.5) + bias_ref[h]
            if a == 0:
                s = jnp.where(jnp.logical_and(from_prev_tile, i == 0), NEG_MASK, s)
            row.append(s)
        logits.append(row)
        vals.append(vv)
    for a in range(n_sub):
        outs, lses = [], []
        for h in range(hq):
            g = h // rep
            gs = slice(g * HEAD_DIM, (g + 1) * HEAD_DIM)
            s = logits[a][h]
            m = jnp.max(s, axis=1, keepdims=True)
            if has_sink:
                m = jnp.maximum(m, sink_ref[h])
            e = jnp.exp(s - m)
            den = jnp.sum(e, axis=1, keepdims=True)
            if has_sink:
                den = den + jnp.exp(sink_ref[h] - m)
            o = jnp.dot(e.astype(bf16), vals[a][:, gs], preferred_element_type=f32) / den
            outs.append(o)
            if not has_sink:
                lses.append(m + jnp.log(den))
        rows = slice(a * BAND, (a + 1) * BAND)
        o_ref[0, rows, :] = jnp.concatenate(outs, axis=1).astype(o_ref.dtype)
        if not has_sink:
            lse_ref[0, rows, :] = jnp.concatenate(lses, axis=1)


def _band_call(name, q, k, v, bias, sinks):
    bn, sn, qw = q.shape
    kw = k.shape[2]
    hq, hk = qw // HEAD_DIM, kw // HEAD_DIM
    tb = min(BAND_TILE, sn)
    per = tb // BAND
    has_sink = sinks is not None
    cur = lambda w: pl.BlockSpec((1, tb, w), lambda b, i: (b, i, 0))
    prev = lambda w: pl.BlockSpec((1, BAND, w), lambda b, i: (b, jnp.maximum(i * per - 1, 0), 0))
    in_specs = [cur(qw), prev(kw), cur(kw), prev(kw), cur(kw), _const_spec(bias.shape)]
    args = [q, k, k, v, v, bias]
    if has_sink:
        in_specs = [pl.BlockSpec(memory_space=pltpu.SMEM)] + in_specs
        args = [sinks.astype(f32)] + args
        out_specs = cur(qw)
        out_shape = jax.ShapeDtypeStruct((bn, sn, qw), bf16)
    else:
        out_specs = [cur(qw), cur(hq)]
        out_shape = [jax.ShapeDtypeStruct((bn, sn, qw), f32), jax.ShapeDtypeStruct((bn, sn, hq), f32)]
    return pl.pallas_call(
        functools.partial(_band_kernel, hq=hq, hk=hk, has_sink=has_sink, tb=tb), name=name,
        grid=(bn, sn // tb), in_specs=in_specs, out_specs=out_specs, out_shape=out_shape,
        compiler_params=_params(2),
    )(*args)


def _merge_kernel(x_ref, shift_ref, scale_ref, gate_ref, oa_ref, ob_ref, oc_refs, lse_refs, od_ref,
                  wg_ref, wb_ref, wo_ref, g_ref, b_ref, o_ref):
    x = x_ref[0]
    u = _modulate(x_ref, shift_ref, scale_ref).astype(bf16)
    lse = [r[0] for r in lse_refs]
    top = jnp.maximum(jnp.maximum(lse[0], lse[1]), lse[2])
    ws = [jnp.exp(v - top) for v in lse]
    ws = [jnp.concatenate([jnp.broadcast_to(w[:, h:h + 1], (T, HEAD_DIM)) for h in range(C_HEADS)], axis=1)
          for w in ws]
    o_c = (ws[0] * oc_refs[0][0] + ws[1] * oc_refs[1][0] + ws[2] * oc_refs[2][0]) / (ws[0] + ws[1] + ws[2])
    contract_first = (((0,), (0,)), ((), ()))
    branch = (
        lax.dot_general(oa_ref[0, 0], wb_ref[0], contract_first, preferred_element_type=f32),
        jnp.dot(ob_ref[0], wb_ref[1], preferred_element_type=f32),
        jnp.dot(o_c.astype(bf16), wb_ref[2], preferred_element_type=f32),
        lax.dot_general(od_ref[0, 0], wb_ref[3], contract_first, preferred_element_type=f32),
    )
    merged = None
    for n in range(N_BRANCH):
        logits = jnp.dot(u, wg_ref[:, n * D_MODEL:(n + 1) * D_MODEL], preferred_element_type=f32)
        term = jax.nn.sigmoid(logits) * branch[n]
        merged = term if merged is None else merged + term
    y = jnp.dot(merged.astype(bf16), wo_ref[...], preferred_element_type=f32)
    z = DEEPNORM_ALPHA * x + (1.0 + gate_ref[0]) * y
    o_ref[0] = _layer_norm(z, g_ref[...], b_ref[...])


def _merge_call(x, shift, scale, gate, o_a, o_b, o_c, lse_c, o_d, w_gate, w_branch, w_out, ln_g, ln_b):
    bsz, seq, d = x.shape
    nt = seq // T
    tok = lambda w: pl.BlockSpec((1, T, w), lambda b, i: (b, i, 0))
    feat = lambda w: pl.BlockSpec((1, 1, w, T), lambda b, i: (b, i, 0, 0))
    mod = pl.BlockSpec((1, 1, d), lambda b, i: (b, 0, 0))

    def kernel(*refs):
        (x_ref, s_ref, c_ref, g_ref, oa, ob, c0, c1, c2, l0, l1, l2, od, wg, wb, wo, lg, lb, o) = refs
        _merge_kernel(x_ref, s_ref, c_ref, g_ref, oa, ob, (c0, c1, c2), (l0, l1, l2), od, wg, wb, wo, lg, lb, o)

    return pl.pallas_call(
        kernel, name="merge_outproj_ln1",
        grid=(bsz, nt),
        in_specs=[tok(d), mod, mod, mod, feat(BRANCH_WIDTH), tok(BRANCH_WIDTH)]
                 + [tok(BRANCH_WIDTH)] * 3 + [tok(C_HEADS)] * 3 + [feat(BRANCH_WIDTH),
                 _const_spec(w_gate.shape), _const_spec(w_branch.shape), _const_spec(w_out.shape),
                 _const_spec((1, d)), _const_spec((1, d))],
        out_specs=tok(d),
        out_shape=jax.ShapeDtypeStruct((bsz, seq, d), f32),
        compiler_params=_params(2),
    )(x, shift, scale, gate, o_a, o_b, *o_c, *lse_c, o_d, w_gate, w_branch, w_out,
      ln_g.reshape(1, d), ln_b.reshape(1, d))


ROUTER_TILE = 512


def _router_kernel(x_ref, shift_ref, scale_ref, wr_ref, br_ref, o_ref):
    v = _modulate(x_ref, shift_ref, scale_ref)
    logits = lax.dot_general(wr_ref[...], v, (((1,), (1,)), ((), ())), preferred_element_type=f32,
                             precision=lax.Precision.HIGHEST)
    e = jnp.exp(logits - jnp.max(logits, axis=0, keepdims=True))
    probs = e / jnp.sum(e, axis=0, keepdims=True)
    score = probs + br_ref[...]
    p = [probs[n:n + 1] for n in range(N_EXPERTS)]
    s = [score[n:n + 1] for n in range(N_EXPERTS)]
    best_g = None
    for g in range(N_GROUPS):
        a, b, c, d = s[g * EXPERTS_PER_GROUP:(g + 1) * EXPERTS_PER_GROUP]
        hi1, lo1, hi2, lo2 = jnp.maximum(a, b), jnp.minimum(a, b), jnp.maximum(c, d), jnp.minimum(c, d)
        gs = jnp.maximum(hi1, hi2) + jnp.maximum(jnp.minimum(hi1, hi2), jnp.maximum(lo1, lo2))
        if best_g is None:
            best_g, grp = gs, jnp.zeros_like(gs, dtype=i32)
        else:
            better = gs > best_g
            best_g = jnp.where(better, gs, best_g)
            grp = jnp.where(better, g, grp)
    neg_inf = jnp.float32(-jnp.inf)
    cand = [jnp.where(grp == n // EXPERTS_PER_GROUP, s[n], neg_inf) for n in range(N_EXPERTS)]

    def first_argmax(vals):
        best, idx = vals[0], jnp.zeros_like(grp)
        for n in range(1, N_EXPERTS):
            better = vals[n] > best
            best = jnp.where(better, vals[n], best)
            idx = jnp.where(better, n, idx)
        return idx

    i1 = first_argmax(cand)
    i2 = first_argmax([jnp.where(i1 == n, neg_inf, cand[n]) for n in range(N_EXPERTS)])
    w1 = sum(jnp.where(i1 == n, p[n], 0.0) for n in range(N_EXPERTS))
    w2 = sum(jnp.where(i2 == n, p[n], 0.0) for n in range(N_EXPERTS))
    tot = w1 + w2
    rows = [jnp.where(i1 == n, w1 / tot, jnp.where(i2 == n, w2 / tot, 0.0)) for n in range(N_EXPERTS)]
    o_ref[0] = jnp.concatenate(rows, axis=0)


def _router_call(x, shift, scale, w_router, b_router):
    bsz, seq, d = x.shape
    rt = min(ROUTER_TILE, seq)
    mod = pl.BlockSpec((1, 1, d), lambda b, i: (b, 0, 0))
    return pl.pallas_call(
        _router_kernel, name="moe_router",
        grid=(bsz, seq // rt),
        in_specs=[pl.BlockSpec((1, rt, d), lambda b, i: (b, i, 0)), mod, mod,
                  _const_spec((N_EXPERTS, d)), _const_spec((N_EXPERTS, 1))],
        out_specs=pl.BlockSpec((1, N_EXPERTS, rt), lambda b, i: (b, 0, i)),
        out_shape=jax.ShapeDtypeStruct((bsz, N_EXPERTS, seq), f32),
        compiler_params=_params(2),
    )(x, shift, scale, w_router.T.astype(f32), b_router.astype(f32).reshape(N_EXPERTS, 1))


GROUP_CAP = 96
MOE_TILE = 512


def _expert_mlp(v, w_cols, experts, wg_ref, wu_ref, wd_ref):
    parts = []
    for w_n, n in zip(w_cols, experts):
        hg = jnp.dot(v, wg_ref[n], preferred_element_type=f32)
        hu = jnp.dot(v, wu_ref[n], preferred_element_type=f32)
        act = hg * jax.nn.sigmoid(hg) * hu
        parts.append(jnp.where(w_n != 0.0, act * w_n, 0.0).astype(bf16))
    h = jnp.concatenate(parts, axis=1)
    rows = pl.ds(experts[0] * D_FF_EXPERT, len(experts) * D_FF_EXPERT)
    return jnp.dot(h, wd_ref[rows, :], preferred_element_type=f32)


def _moe_kernel(x_ref, shift_ref, scale_ref, gate_ref, wt_ref, wtt_ref, wg_ref, wu_ref, wd_ref,
                g_ref, b_ref, o_ref):
    n_sub = x_ref.shape[1] // T
    x = x_ref[0]
    v = _modulate(x_ref, shift_ref, scale_ref).astype(bf16)
    wt = wt_ref[0]
    wt_t = wtt_ref[0]

    member_f = jnp.concatenate(
        [jnp.where(jnp.sum(wt_t[g * EXPERTS_PER_GROUP:(g + 1) * EXPERTS_PER_GROUP], axis=0, keepdims=True)
                   > 0.0, 1.0, 0.0) for g in range(N_GROUPS)]
        + [jnp.zeros((SUBLANES - N_GROUPS, n_sub * T), f32)], axis=0)
    earlier = (lax.broadcasted_iota(i32, (T, T), 0) < lax.broadcasted_iota(i32, (T, T), 1))
    earlier_b = jnp.where(earlier, 1.0, 0.0).astype(bf16)
    subs = [slice(n * T, (n + 1) * T) for n in range(n_sub)]
    rank = [jnp.dot(member_f[:, sub].astype(bf16), earlier_b, preferred_element_type=f32)
            for sub in subs]
    count = functools.reduce(jnp.maximum, [jnp.max(r + member_f[:, sub]) for r, sub in zip(rank, subs)])

    def grouped():
        slot = lax.broadcasted_iota(i32, (GROUP_CAP, T), 0).astype(f32)
        back = (((0,), (0,)), ((), ()))
        y = [jnp.zeros((T, D_MODEL), f32) for _ in subs]
        for g in range(N_GROUPS):
            experts = list(range(g * EXPERTS_PER_GROUP, (g + 1) * EXPERTS_PER_GROUP))
            picks, rows, weights = [], [], []
            for r, sub in zip(rank, subs):
                pick = jnp.logical_and(slot == r[g:g + 1], member_f[g:g + 1, sub] > 0.5)
                pick_f = jnp.where(pick, 1.0, 0.0)
                picks.append(pick_f.astype(bf16))
                rows.append(jnp.dot(picks[-1], v[sub], preferred_element_type=f32).astype(bf16))
                weights.append(jnp.dot(pick_f, wt[sub, experts[0]:experts[-1] + 1],
                                       preferred_element_type=f32, precision=lax.Precision.HIGHEST))
            v_g = jnp.concatenate(rows, axis=0)
            w_g = jnp.concatenate(weights, axis=0)
            y_g = _expert_mlp(v_g, [w_g[:, k:k + 1] for k in range(EXPERTS_PER_GROUP)], experts,
                              wg_ref, wu_ref, wd_ref)
            y_hi = y_g.astype(bf16)
            y_lo = (y_g - y_hi.astype(f32)).astype(bf16)
            for n, pick_b in enumerate(picks):
                part = slice(n * GROUP_CAP, (n + 1) * GROUP_CAP)
                y[n] = (y[n] + lax.dot_general(pick_b, y_hi[part], back, preferred_element_type=f32)
                        + lax.dot_general(pick_b, y_lo[part], back, preferred_element_type=f32))
        return jnp.concatenate(y, axis=0)

    def dense():
        experts = list(range(N_EXPERTS))
        return _expert_mlp(v, [wt[:, n:n + 1] for n in experts], experts, wg_ref, wu_ref, wd_ref)

    y = lax.cond(count <= GROUP_CAP, grouped, dense)
    z = DEEPNORM_ALPHA * x + (1.0 + gate_ref[0]) * y
    o_ref[0] = _layer_norm(z, g_ref[...], b_ref[...])


def _moe_call(x, shift, scale, gate, wt_t, w_gate, w_up, w_down, ln_g, ln_b):
    bsz, seq, d = x.shape
    mt = MOE_TILE
    tok = lambda w: pl.BlockSpec((1, mt, w), lambda b, i: (b, i, 0))
    mod = pl.BlockSpec((1, 1, d), lambda b, i: (b, 0, 0))
    return pl.pallas_call(
        _moe_kernel, name="moe_experts_ln2",
        grid=(bsz, seq // mt),
        in_specs=[tok(d), mod, mod, mod, tok(N_EXPERTS),
                  pl.BlockSpec((1, N_EXPERTS, mt), lambda b, i: (b, 0, i)),
                  _const_spec(w_gate.shape), _const_spec(w_up.shape), _const_spec(w_down.shape),
                  _const_spec((1, d)), _const_spec((1, d))],
        out_specs=tok(d),
        out_shape=jax.ShapeDtypeStruct((bsz, seq, d), f32),
        compiler_params=_params(2),
    )(x, shift, scale, gate, jnp.swapaxes(wt_t, 1, 2), wt_t, w_gate, w_up, w_down,
      ln_g.reshape(1, d), ln_b.reshape(1, d))


def _t5_bucket(n):
    max_exact = REL_BUCKETS // 2
    nf = jnp.maximum(n, 1).astype(f32)
    log_ratio = jnp.log(nf / max_exact) / math.log(REL_MAX_DIST / max_exact)
    large = max_exact + (log_ratio * (REL_BUCKETS - max_exact)).astype(i32)
    large = jnp.minimum(large, REL_BUCKETS - 1)
    return jnp.where(n < max_exact, n, large)


def _toeplitz_kernel(v_ref, o_ref):
    rows, width = o_ref.shape[1], o_ref.shape[2]
    x = jnp.broadcast_to(v_ref[0], (rows, width))
    o_ref[0] = pltpu.roll(x, 0, 1, stride=1, stride_axis=0)


def _toeplitz_call(vec, rows):
    heads, width = vec.shape
    return pl.pallas_call(
        _toeplitz_kernel, name="bias_toeplitz",
        grid=(heads,),
        in_specs=[pl.BlockSpec((1, 1, width), lambda h: (h, 0, 0))],
        out_specs=pl.BlockSpec((1, rows, width), lambda h: (h, 0, 0)),
        out_shape=jax.ShapeDtypeStruct((heads, rows, width), f32),
        compiler_params=_params(1),
    )(vec.reshape(heads, 1, width))


def _bias_tables(rel_table, seq):
    n_cols = (N_FAR + 1) * T
    n_dist = max(seq, n_cols)
    by_dist = rel_table.astype(f32)[_t5_bucket(jnp.arange(n_dist, dtype=i32))].T
    dense_heads = np.r_[A_BIAS0:A_BIAS0 + A_HEADS, D_BIAS0:D_BIAS0 + D_HEADS]
    dense = _toeplitz_call(by_dist[dense_heads, :n_cols] * LOG2E, T)
    dense = dense.reshape(len(dense_heads), T, N_FAR + 1, T).transpose(0, 2, 1, 3)
    bias_a = dense[:A_HEADS]
    bias_d = dense[A_HEADS:]
    width = 4 * BAND

    def band_vec(h0, nh, max_dist, dilation):
        vals = by_dist[h0:h0 + nh, 0:(BAND + 1) * dilation:dilation]
        vals = jnp.where(np.arange(BAND + 1) <= max_dist, vals, NEG_MASK)[:, ::-1]
        return jnp.pad(vals, ((0, 0), (BAND - 1, width - 2 * BAND)), constant_values=NEG_MASK)

    vecs = [band_vec(B_BIAS0, B_QHEADS, B_WINDOW - 1, 1)]
    vecs += [band_vec(C_BIAS0 + g * C_HEADS, C_HEADS, window // dil, dil)
             for g, (window, dil) in enumerate(C_PATTERNS)]
    band = _toeplitz_call(jnp.concatenate(vecs, axis=0), BAND)[:, :, BAND - 1:3 * BAND - 1]
    bias_b = band[:B_QHEADS]
    bias_c = [band[B_QHEADS + g * C_HEADS:B_QHEADS + (g + 1) * C_HEADS] for g in range(len(C_PATTERNS))]
    return bias_a, bias_b, bias_c, bias_d


def _split_cols(w):
    parts, off = {}, 0
    for name, width in IN_SPLITS:
        parts[name] = w[:, off:off + width]
        off += width
    return parts


def _dilate(a, r):
    if r == 1:
        return a
    b, s, f = a.shape
    return a.reshape(b, s // r, r, f).transpose(0, 2, 1, 3).reshape(b * r, s // r, f)


def _undilate(a, r, bsz):
    if r == 1:
        return a
    _, sn, f = a.shape
    return a.reshape(bsz, r, sn, f).transpose(0, 2, 1, 3).reshape(bsz, sn * r, f)


def _layer(l, x, mod, tables, w_router, b_router, w_in, lam_q, lam_k, subln_g, sinks, w_branch, w_out,
           ln1_g, ln1_b, w_exp_gate, w_exp_up, w_exp_down, ln2_g, ln2_b):
    bsz, seq, d = x.shape
    shift_a, scale_a, gate_a, shift_f, scale_f, gate_f = mod
    bias_a, bias_b, bias_c, bias_d = tables
    w = _split_cols(w_in.astype(bf16))
    hw = C_HEADS * HEAD_DIM

    aq, ak, av = _proj_call("proj_a", x, shift_a, scale_a, [w['a_q'], w['a_k'], w['a_v']],
                            [('feat', A_QK ** -0.5 * LOG2E), ('tokh', 2 * A_HEADS, A_QK), ('feat',)],
                            [bf16] * 3)
    o_a = _diff_attn_call(aq, ak, av, bias_a, lam_q, lam_k, subln_g, l)

    bq, bk, bv = _proj_call("proj_b", x, shift_a, scale_a, [w['b_q'], w['b_k'], w['b_v']],
                            [('tok',)] * 3, [bf16] * 3)
    o_b = _band_call("mixer_b_window_attn", bq, bk, bv, bias_b, sinks)

    c_cols = [w[n][:, g * hw:(g + 1) * hw] for g in range(len(C_PATTERNS)) for n in ('c_q', 'c_k', 'c_v')]
    c_proj = _proj_call("proj_c", x, shift_a, scale_a, c_cols, [('tok',)] * 9, [bf16] * 9)
    o_c, lse_c = [], []
    for g, (_, dil) in enumerate(C_PATTERNS):
        qg, kg, vg = (_dilate(a, dil) for a in c_proj[3 * g:3 * g + 3])
        og, lg = _band_call(f"mixer_c_dilated_attn_{g}", qg, kg, vg, bias_c[g], None)
        o_c.append(_undilate(og, dil, bsz))
        lse_c.append(_undilate(lg, dil, bsz))

    dq, dk, dv, iq, ik, iw = _proj_call(
        "proj_d", x, shift_a, scale_a, [w['d_q'], w['d_k'], w['d_v'], w['i_q'], w['i_k'], w['i_w']],
        [('feat', HEAD_DIM ** -0.5 * LOG2E), ('tok',), ('feat',), ('feat',), ('tok',), ('feat',)],
        [bf16] * 5 + [f32])
    o_d = _sparse_attn_call(dq, dk, dv, iq, ik, iw, bias_d)

    x1 = _merge_call(x, shift_a, scale_a, gate_a, o_a, o_b, o_c, lse_c, o_d, w['gate'],
                     w_branch.astype(bf16), w_out.astype(bf16), ln1_g, ln1_b)

    wt = _router_call(x1, shift_f, scale_f, w_router, b_router)
    wd_all = w_exp_down.reshape(N_EXPERTS * D_FF_EXPERT, d).astype(bf16)
    return _moe_call(x1, shift_f, scale_f, gate_f, wt, w_exp_gate.astype(bf16), w_exp_up.astype(bf16),
                     wd_all, ln2_g, ln2_b)


def kernel(x, c, rel_table, w_router, b_router, w_ada, b_ada, w_in, a_lambda_q, a_lambda_k, a_subln_g,
           b_sinks, w_branch, w_out, ln1_g, ln1_b, w_exp_gate, w_exp_up, w_exp_down, ln2_g, ln2_b):
    bsz, seq, d = x.shape
    assert d == D_MODEL and seq % (BAND * C_PATTERNS[-1][1]) == 0 and seq % BAND_TILE == 0
    mod = _ada_call(c, w_ada, b_ada)
    tables = _bias_tables(rel_table, seq)
    for l in range(DEPTH):
        mod_l = tuple(m.reshape(bsz, 1, d) for m in jnp.split(mod[l], 6, axis=-1))
        x = _layer(l, x, mod_l, tables, w_router, b_router, w_in[l], a_lambda_q[l], a_lambda_k[l],
                   a_subln_g[l], b_sinks[l], w_branch[l], w_out[l], ln1_g[l], ln1_b[l],
                   w_exp_gate[l], w_exp_up[l], w_exp_down[l], ln2_g[l], ln2_b[l])
    return x
```
